```python
import math
import jax, jax.numpy as jnp
from jax import lax
import numpy as np

D_MODEL = 2048
BATCH = 2
SEQ = 8192
DEPTH = 2

MIX_WIDTH = D_MODEL
ATT_WIDTH = MIX_WIDTH // 2
REC_WIDTH = MIX_WIDTH - ATT_WIDTH
ATT_HEAD_DIM = 64
ATT_V_DIM = 2 * ATT_HEAD_DIM
ATT_HEADS = ATT_WIDTH // ATT_V_DIM
REC_EXPAND = 128
REC_HEADS = REC_WIDTH // REC_EXPAND
REC_K_DIM = REC_EXPAND
REC_V_DIM = REC_WIDTH // REC_HEADS
D_FF = 5504
ROPE_THETA = 10000.0
NORM_EPS = 1e-6
Q_BLOCK = 128
CHUNK = 64

IN_SIZES = (
    ATT_HEADS * 2 * ATT_HEAD_DIM,
    ATT_HEADS * 2 * ATT_HEAD_DIM,
    ATT_HEADS * ATT_V_DIM,
    REC_HEADS * REC_K_DIM,
    REC_HEADS * REC_K_DIM,
    REC_HEADS * REC_V_DIM,
    REC_HEADS * REC_V_DIM,
)
IN_COLS = sum(IN_SIZES)
IN_SPLITS = tuple(int(v) for v in np.cumsum(IN_SIZES)[:-1])

kernel_name = "hybrid_diffattn_hgrn2_macaron"


def rmsnorm(x, w):
    xf = x.astype(jnp.float32)
    y = xf * lax.rsqrt(jnp.mean(xf * xf, axis=-1, keepdims=True) + NORM_EPS)
    return (y * w.astype(jnp.float32)).astype(x.dtype)


def swiglu(h, w_gate, w_up, w_down):
    return (jax.nn.silu(h @ w_gate) * (h @ w_up)) @ w_down


def rope(t, pos):
    d = t.shape[-1]
    inv_freq = ROPE_THETA ** (-jnp.arange(0, d, 2, dtype=jnp.float32) / d)
    ang = pos[:, None] * inv_freq[None, :]
    ang = jnp.concatenate([ang, ang], axis=-1)[None, :, None, :]
    tf = t.astype(jnp.float32)
    t1, t2 = jnp.split(tf, 2, axis=-1)
    rot = jnp.concatenate([-t2, t1], axis=-1)
    return (tf * jnp.cos(ang) + rot * jnp.sin(ang)).astype(t.dtype)


def diff_attention(q, k, v, lam, lambda_init, subln_w):
    B, S, H = q.shape[0], q.shape[1], q.shape[2]
    q = q.transpose(0, 3, 2, 1, 4)
    k = k.transpose(0, 3, 2, 1, 4)
    vt = v.transpose(0, 2, 1, 3)
    scale = ATT_HEAD_DIM ** -0.5
    kpos = jnp.arange(S)

    def one_block(blk):
        start = blk * Q_BLOCK
        qb = lax.dynamic_slice_in_dim(q, start, Q_BLOCK, axis=3)
        s = jnp.einsum('bchqd,bchkd->bchqk', qb, k,
                       preferred_element_type=jnp.float32) * scale
        qpos = start + jnp.arange(Q_BLOCK)
        mask = kpos[None, :] <= qpos[:, None]
        p = jax.nn.softmax(jnp.where(mask, s, -jnp.inf), axis=-1)
        a = p[:, 0] - lam * p[:, 1]
        return jnp.einsum('bhqk,bhkd->bhqd', a.astype(vt.dtype), vt)

    out = lax.map(one_block, jnp.arange(S // Q_BLOCK))
    out = out.transpose(1, 0, 3, 2, 4).reshape(B, S, H, ATT_V_DIM)
    out = rmsnorm(out, subln_w) * (1.0 - lambda_init)
    return out.reshape(B, S, H * ATT_V_DIM)


def hgrn2(q, f_logit, i, g, lb, gnorm_w):
    B, S, H, K = q.shape
    V = i.shape[-1]
    n_chunks = S // CHUNK
    qf = jax.nn.silu(q.astype(jnp.float32))
    fl = f_logit.astype(jnp.float32)
    lb = lb.astype(jnp.float32)
    log_f = jnp.log(lb + (1.0 - lb) * jax.nn.sigmoid(fl))
    kk = (1.0 - lb) * jax.nn.sigmoid(-fl)
    vv = i.astype(jnp.float32)

    def to_chunks(t):
        return t.reshape(B, n_chunks, CHUNK, H, t.shape[-1]).transpose(1, 0, 3, 2, 4)

    tri = jnp.tril(jnp.ones((CHUNK, CHUNK), dtype=bool))[:, :, None]

    def step(state, inp):
        qc, kc, vc, lfc = inp
        b = jnp.cumsum(lfc, axis=2)
        o_inter = jnp.einsum('bhtk,bhkv->bhtv', qc * jnp.exp(b), state)
        rel = b[:, :, :, None, :] - b[:, :, None, :, :]
        decay = jnp.where(tri, jnp.exp(jnp.where(tri, rel, 0.0)), 0.0)
        scores = jnp.einsum('bhtsk,bhsk->bhts', qc[:, :, :, None, :] * decay, kc)
        o_intra = jnp.einsum('bhts,bhsv->bhtv', scores, vc)
        b_last = b[:, :, -1:, :]
        new_state = (jnp.exp(b_last[:, :, 0, :])[..., None] * state
                     + jnp.einsum('bhsk,bhsv->bhkv', kc * jnp.exp(b_last - b), vc))
        return new_state, o_inter + o_intra

    state0 = jnp.zeros((B, H, K, V), jnp.float32)
    _, o = lax.scan(step, state0, (to_chunks(qf), to_chunks(kk), to_chunks(vv), to_chunks(log_f)))
    o = o.transpose(1, 0, 3, 2, 4).reshape(B, S, H, V)
    o = rmsnorm(o, gnorm_w).reshape(B, S, H * V)
    o = o * jax.nn.silu(g.astype(jnp.float32))
    return o.astype(g.dtype)


def setup_inputs(seed: int = 0) -> dict:
    key = jax.random.key(seed)
    ks = jax.random.split(key, 24)
    f32 = jnp.float32

    def nrm(k, shape, scale):
        return jax.random.normal(k, shape, f32) * scale

    def gain(k, shape):
        return 1.0 + 0.01 * jax.random.normal(k, shape, f32)

    return {
        "x": jax.random.normal(ks[0], (BATCH, SEQ, D_MODEL), f32),
        "ffn1_norm": gain(ks[1], (DEPTH, D_MODEL)),
        "ffn1_w_gate": nrm(ks[2], (DEPTH, D_MODEL, D_FF), D_MODEL ** -0.5),
        "ffn1_w_up": nrm(ks[3], (DEPTH, D_MODEL, D_FF), D_MODEL ** -0.5),
        "ffn1_w_down": nrm(ks[4], (DEPTH, D_FF, D_MODEL), D_FF ** -0.5),
        "mix_norm": gain(ks[5], (DEPTH, D_MODEL)),
        "w_in": nrm(ks[6], (DEPTH, D_MODEL, IN_COLS), D_MODEL ** -0.5),
        "w_out": nrm(ks[7], (DEPTH, MIX_WIDTH, D_MODEL), MIX_WIDTH ** -0.5),
        "lambda_q1": nrm(ks[8], (DEPTH, ATT_HEAD_DIM), 0.1),
        "lambda_k1": nrm(ks[9], (DEPTH, ATT_HEAD_DIM), 0.1),
        "lambda_q2": nrm(ks[10], (DEPTH, ATT_HEAD_DIM), 0.1),
        "lambda_k2": nrm(ks[11], (DEPTH, ATT_HEAD_DIM), 0.1),
        "diff_subln": gain(ks[12], (DEPTH, ATT_V_DIM)),
        "hgrn_lower_bounds": nrm(ks[13], (DEPTH, REC_HEADS * REC_K_DIM), 0.1),
        "hgrn_gnorm": gain(ks[14], (DEPTH, REC_V_DIM)),
        "ffn2_norm": gain(ks[15], (DEPTH, D_MODEL)),
        "ffn2_w_gate": nrm(ks[16], (DEPTH, D_MODEL, D_FF), D_MODEL ** -0.5),
        "ffn2_w_up": nrm(ks[17], (DEPTH, D_MODEL, D_FF), D_MODEL ** -0.5),
        "ffn2_w_down": nrm(ks[18], (DEPTH, D_FF, D_MODEL), D_FF ** -0.5),
        "final_norm": gain(ks[19], (D_MODEL,)),
    }


def reference(x, ffn1_norm, ffn1_w_gate, ffn1_w_up, ffn1_w_down, mix_norm, w_in, w_out,
              lambda_q1, lambda_k1, lambda_q2, lambda_k2, diff_subln, hgrn_lower_bounds,
              hgrn_gnorm, ffn2_norm, ffn2_w_gate, ffn2_w_up, ffn2_w_down, final_norm):
    B, S, _ = x.shape
    pos = jnp.arange(S, dtype=jnp.float32)
    lbs = jax.nn.softmax(hgrn_lower_bounds.astype(jnp.float32), axis=0)
    lbs = jnp.cumsum(lbs, axis=0) - lbs[0]

    for l in range(DEPTH):
        h = rmsnorm(x, ffn1_norm[l])
        x = x + 0.5 * swiglu(h, ffn1_w_gate[l], ffn1_w_up[l], ffn1_w_down[l])

        h = rmsnorm(x, mix_norm[l])
        proj = h @ w_in[l]
        aq, ak, av, rq, rf, ri, rg = jnp.split(proj, IN_SPLITS, axis=-1)

        aq = rope(aq.reshape(B, S, 2 * ATT_HEADS, ATT_HEAD_DIM), pos).reshape(B, S, ATT_HEADS, 2, ATT_HEAD_DIM)
        ak = rope(ak.reshape(B, S, 2 * ATT_HEADS, ATT_HEAD_DIM), pos).reshape(B, S, ATT_HEADS, 2, ATT_HEAD_DIM)
        av = av.reshape(B, S, ATT_HEADS, ATT_V_DIM)
        lambda_init = 0.8 - 0.6 * math.exp(-0.3 * l)
        lam = (jnp.exp(jnp.sum(lambda_q1[l].astype(jnp.float32) * lambda_k1[l].astype(jnp.float32)))
               - jnp.exp(jnp.sum(lambda_q2[l].astype(jnp.float32) * lambda_k2[l].astype(jnp.float32)))
               + lambda_init)
        att_out = diff_attention(aq, ak, av, lam, lambda_init, diff_subln[l])

        rec_out = hgrn2(rq.reshape(B, S, REC_HEADS, REC_K_DIM),
                        rf.reshape(B, S, REC_HEADS, REC_K_DIM),
                        ri.reshape(B, S, REC_HEADS, REC_V_DIM),
                        rg,
                        lbs[l].reshape(REC_HEADS, REC_K_DIM),
                        hgrn_gnorm[l])

        mix = jnp.concatenate([att_out, rec_out.astype(att_out.dtype)], axis=-1)
        x = x + mix @ w_out[l]

        h = rmsnorm(x, ffn2_norm[l])
        x = x + 0.5 * swiglu(h, ffn2_w_gate[l], ffn2_w_up[l], ffn2_w_down[l])

    return rmsnorm(x, final_norm)
```

```python
import functools
import math

import jax
import jax.numpy as jnp
import numpy as np
from jax import lax
from jax.experimental import pallas as pl
from jax.experimental.pallas import tpu as pltpu

NORM_EPS = 1e-6
ROPE_THETA = 10000.0
ATT_HEAD_DIM = 64
HEAD_WIDTH = 128
N_IN_PARTS = 7
LANES = 128
V7X_VMEM_LIMIT_BYTES = 60 * 1024 * 1024

FFN_TOKENS = 512
FFN_HIDDEN_BLOCK = 512
PROJ_TOKENS = 512
OUT_TOKENS = 512
ATT_Q_BLOCK = 256
ATT_KV_BLOCK = 512
REC_BLOCK = 128

F32 = jnp.float32
BF16 = jnp.bfloat16


def _rms(x, w):
    return x * lax.rsqrt(jnp.mean(x * x, axis=-1, keepdims=True) + NORM_EPS) * w


def _silu(x):
    return x * jax.nn.sigmoid(x)


def _params(semantics):
    return pltpu.CompilerParams(dimension_semantics=semantics,
                                vmem_limit_bytes=V7X_VMEM_LIMIT_BYTES)


def _ffn_kernel(x_ref, nw_ref, wg_ref, wu_ref, wd_ref, *rest, final):
    if final:
        fw_ref, o_ref, hn_ref = rest
    else:
        o_ref, hn_ref = rest
    j = pl.program_id(1)

    @pl.when(j == 0)
    def _():
        x = x_ref[...]
        hn_ref[...] = _rms(x, nw_ref[...]).astype(BF16)
        o_ref[...] = x

    h = hn_ref[...]
    g = jnp.dot(h, wg_ref[...], preferred_element_type=F32)
    u = jnp.dot(h, wu_ref[...], preferred_element_type=F32)
    a = (0.5 * _silu(g) * u).astype(BF16)
    o_ref[...] += jnp.dot(a, wd_ref[...], preferred_element_type=F32)

    if final:
        @pl.when(j == pl.num_programs(1) - 1)
        def _():
            o_ref[...] = _rms(o_ref[...], fw_ref[...])


def _ffn(x, norm_w, wg, wu, wd, final_w=None):
    n, d = x.shape
    f = wg.shape[1]
    tm = min(FFN_TOKENS, n)
    tf = FFN_HIDDEN_BLOCK
    assert n % tm == 0 and f % tf == 0
    final = final_w is not None
    row = lambda i, j: (i, 0)
    vec = pl.BlockSpec((1, d), lambda i, j: (0, 0))
    in_specs = [pl.BlockSpec((tm, d), row), vec,
                pl.BlockSpec((d, tf), lambda i, j: (0, j)),
                pl.BlockSpec((d, tf), lambda i, j: (0, j)),
                pl.BlockSpec((tf, d), lambda i, j: (j, 0))]
    args = [x, norm_w.reshape(1, d), wg, wu, wd]
    if final:
        in_specs.append(vec)
        args.append(final_w.reshape(1, d))
    return pl.pallas_call(
        functools.partial(_ffn_kernel, final=final),
        out_shape=jax.ShapeDtypeStruct((n, d), F32),
        grid=(n // tm, f // tf),
        in_specs=in_specs,
        out_specs=pl.BlockSpec((tm, d), row),
        scratch_shapes=[pltpu.VMEM((tm, d), BF16)],
        compiler_params=_params(("parallel", "arbitrary")),
        name="ffn_final" if final else "ffn",
    )(*args)


def _rope(y, cos, sin_lo, sin_hi):
    width = y.shape[1]
    from_hi = pltpu.roll(y, width - ATT_HEAD_DIM // 2, 1)
    from_lo = pltpu.roll(y, ATT_HEAD_DIM // 2, 1)
    reps = width // LANES
    tile = lambda t: jnp.concatenate([t] * reps, axis=1)
    return y * tile(cos) + from_hi * tile(sin_lo) + from_lo * tile(sin_hi)


def _proj_kernel(x_ref, nw_ref, w_ref, cos_ref, slo_ref, shi_ref, lbraw_ref,
                 q_o, k_o, v_o, rq_o, lf_o, kk_o, ri_o, rg_o, hn_ref, *, layer):
    j = pl.program_id(1)

    @pl.when(j == 0)
    def _():
        hn_ref[...] = _rms(x_ref[...], nw_ref[...]).astype(BF16)

    y = jnp.dot(hn_ref[...], w_ref[...], preferred_element_type=F32)

    @pl.when(j == 0)
    def _():
        scale = ATT_HEAD_DIM ** -0.5
        q_o[...] = (_rope(y, cos_ref[...], slo_ref[...], shi_ref[...]) * scale).astype(BF16)

    @pl.when(j == 1)
    def _():
        k_o[...] = _rope(y, cos_ref[...], slo_ref[...], shi_ref[...]).astype(BF16)

    @pl.when(j == 2)
    def _():
        v_o[...] = y.astype(BF16)

    @pl.when(j == 3)
    def _():
        rq_o[...] = _silu(y)

    @pl.when(j == 4)
    def _():
        raw = lbraw_ref[...]
        e = jnp.exp(raw - jnp.max(raw, axis=0, keepdims=True))
        sm = e / jnp.sum(e, axis=0, keepdims=True)
        lb = jnp.sum(sm[:layer + 1], axis=0, keepdims=True) - sm[0:1]
        lf_o[...] = jnp.log(lb + (1.0 - lb) * jax.nn.sigmoid(y))
        kk_o[...] = (1.0 - lb) * jax.nn.sigmoid(-y)

    @pl.when(j == 5)
    def _():
        ri_o[...] = y.astype(BF16)

    @pl.when(j == 6)
    def _():
        rg_o[...] = _silu(y)


def _proj(x, norm_w, w_in, tables, lb_raw, layer, seq):
    n, d = x.shape
    width = w_in.shape[1] // N_IN_PARTS
    tm = min(PROJ_TOKENS, seq)
    assert n % tm == 0 and seq % tm == 0
    pos_blocks = seq // tm
    depth = lb_raw.shape[0]
    row = lambda i, j: (i, 0)
    tab = pl.BlockSpec((tm, LANES), lambda i, j: (i % pos_blocks, 0))
    out_dtypes = (BF16, BF16, BF16, F32, F32, F32, BF16, F32)
    return pl.pallas_call(
        functools.partial(_proj_kernel, layer=layer),
        out_shape=[jax.ShapeDtypeStruct((n, width), dt) for dt in out_dtypes],
        grid=(n // tm, N_IN_PARTS),
        in_specs=[pl.BlockSpec((tm, d), row),
                  pl.BlockSpec((1, d), lambda i, j: (0, 0)),
                  pl.BlockSpec((d, width), lambda i, j: (0, j)),
                  tab, tab, tab,
                  pl.BlockSpec((depth, width), lambda i, j: (0, 0))],
        out_specs=[pl.BlockSpec((tm, width), row) for _ in out_dtypes],
        scratch_shapes=[pltpu.VMEM((tm, d), BF16)],
        compiler_params=_params(("parallel", "arbitrary")),
        name="in_proj",
    )(x, norm_w.reshape(1, d), w_in, *tables, lb_raw)


def _rope_tables(seq):
    half = ATT_HEAD_DIM // 2
    inv_freq = ROPE_THETA ** (-jnp.arange(0, ATT_HEAD_DIM, 2, dtype=F32) / ATT_HEAD_DIM)
    ang = jnp.arange(seq, dtype=F32)[:, None] * inv_freq[None, :]
    cos, sin, zero = jnp.cos(ang), jnp.sin(ang), jnp.zeros_like(ang)
    reps = LANES // ATT_HEAD_DIM
    cos_t = jnp.tile(jnp.concatenate([cos, cos], axis=1), (1, reps))
    sin_lo = jnp.tile(jnp.concatenate([-sin, zero], axis=1), (1, reps))
    sin_hi = jnp.tile(jnp.concatenate([zero, sin], axis=1), (1, reps))
    del half
    return cos_t, sin_lo, sin_hi


def _att_schedule(seq, tq, tk):
    qi, kj, first, last, diag = [], [], [], [], []
    for i in range(seq // tq):
        jmax = ((i + 1) * tq - 1) // tk
        for j in range(jmax + 1):
            qi.append(i)
            kj.append(j)
            first.append(int(j == 0))
            last.append(int(j == jmax))
            diag.append(int((j + 1) * tk - 1 > i * tq))
    return [np.asarray(a, np.int32) for a in (qi, kj, first, last, diag)]


def _att_kernel(qi_ref, kj_ref, first_ref, last_ref, diag_ref,
                q_ref, k_ref, v_ref, lq1_ref, lk1_ref, lq2_ref, lk2_ref, sw_ref,
                o_ref, qz_ref, m_ref, l_ref, acc_ref, *, heads, tq, tk, lambda_init):
    t = pl.program_id(1)
    qi, kj = qi_ref[t], kj_ref[t]

    @pl.when(first_ref[t] == 1)
    def _():
        m_ref[...] = jnp.full(m_ref.shape, -jnp.inf, F32)
        l_ref[...] = jnp.zeros(l_ref.shape, F32)
        acc_ref[...] = jnp.zeros(acc_ref.shape, F32)
        q = q_ref[...]
        low = lax.broadcasted_iota(jnp.int32, q.shape, 1) % HEAD_WIDTH < ATT_HEAD_DIM
        zero = jnp.zeros_like(q)
        qz_ref[0:tq, :] = jnp.where(low, q, zero)
        qz_ref[tq:2 * tq, :] = jnp.where(low, zero, q)

    def step(masked):
        if masked:
            row = lax.broadcasted_iota(jnp.int32, (2 * tq, tk), 0)
            col = lax.broadcasted_iota(jnp.int32, (2 * tq, tk), 1)
            qpos = qi * tq + jnp.where(row >= tq, row - tq, row)
            keep = kj * tk + col <= qpos
        for h in range(heads):
            cols = slice(h * HEAD_WIDTH, (h + 1) * HEAD_WIDTH)
            s = lax.dot_general(qz_ref[:, cols], k_ref[:, cols], (((1,), (1,)), ((), ())),
                                preferred_element_type=F32)
            if masked:
                s = jnp.where(keep, s, -jnp.inf)
            m_prev = m_ref[h]
            m_new = jnp.maximum(m_prev, jnp.max(s, axis=1, keepdims=True))
            alpha = jnp.exp(m_prev - m_new)
            p = jnp.exp(s - m_new[:, 0:1])
            l_ref[h] = alpha * l_ref[h] + jnp.sum(p, axis=1, keepdims=True)
            acc_ref[h] = alpha * acc_ref[h] + jnp.dot(p.astype(BF16), v_ref[:, cols],
                                                      preferred_element_type=F32)
            m_ref[h] = m_new

    @pl.when(diag_ref[t] == 1)
    def _():
        step(True)

    @pl.when(diag_ref[t] == 0)
    def _():
        step(False)

    @pl.when(last_ref[t] == 1)
    def _():
        lam = (jnp.exp(jnp.sum(lq1_ref[...] * lk1_ref[...], axis=1, keepdims=True))
               - jnp.exp(jnp.sum(lq2_ref[...] * lk2_ref[...], axis=1, keepdims=True))
               + lambda_init)
        for h in range(heads):
            o12 = acc_ref[h] / l_ref[h]
            o = o12[0:tq] - lam * o12[tq:2 * tq]
            o = _rms(o, sw_ref[...]) * (1.0 - lambda_init)
            o_ref[:, h * HEAD_WIDTH:(h + 1) * HEAD_WIDTH] = o.astype(o_ref.dtype)


def _attention(q, k, v, lq1, lk1, lq2, lk2, subln_w, batch, seq, lambda_init):
    n, width = q.shape
    heads = width // HEAD_WIDTH
    tq = min(ATT_Q_BLOCK, seq)
    tk = min(ATT_KV_BLOCK, seq)
    assert seq % tq == 0 and seq % tk == 0
    sched = _att_schedule(seq, tq, tk)
    nq, nk = seq // tq, seq // tk
    vec = lambda m: pl.BlockSpec((1, m), lambda b, t, *_: (0, 0))
    grid_spec = pltpu.PrefetchScalarGridSpec(
        num_scalar_prefetch=len(sched),
        grid=(batch, len(sched[0])),
        in_specs=[pl.BlockSpec((tq, width), lambda b, t, qi, kj, *_: (b * nq + qi[t], 0)),
                  pl.BlockSpec((tk, width), lambda b, t, qi, kj, *_: (b * nk + kj[t], 0)),
                  pl.BlockSpec((tk, width), lambda b, t, qi, kj, *_: (b * nk + kj[t], 0)),
                  vec(ATT_HEAD_DIM), vec(ATT_HEAD_DIM), vec(ATT_HEAD_DIM), vec(ATT_HEAD_DIM),
                  vec(HEAD_WIDTH)],
        out_specs=pl.BlockSpec((tq, width), lambda b, t, qi, kj, *_: (b * nq + qi[t], 0)),
        scratch_shapes=[pltpu.VMEM((2 * tq, width), BF16),
                        pltpu.VMEM((heads, 2 * tq, LANES), F32),
                        pltpu.VMEM((heads, 2 * tq, LANES), F32),
                        pltpu.VMEM((heads, 2 * tq, HEAD_WIDTH), F32)],
    )
    r = lambda a: a.reshape(1, -1)
    return pl.pallas_call(
        functools.partial(_att_kernel, heads=heads, tq=tq, tk=tk, lambda_init=lambda_init),
        out_shape=jax.ShapeDtypeStruct((n, width), BF16),
        grid_spec=grid_spec,
        compiler_params=_params(("parallel", "arbitrary")),
        name="diff_attention",
    )(*sched, q, k, v, r(lq1), r(lk1), r(lq2), r(lk2), r(subln_w))


def _rec_levels(t):
    return [t >> s for s in range(1, t.bit_length())]


def _rec_range_sums(t):
    tok = np.arange(t)
    u = tok[None, :]
    blocks = []
    for w in _rec_levels(t):
        m = (tok // (2 * w)) * (2 * w) + w - 1
        late = (tok % (2 * w)) >= w
        a = np.where(late[:, None], (u > m[:, None]) & (u <= tok[:, None]),
                     (u > tok[:, None]) & (u <= m[:, None]))
        blocks.append(a)
    blocks.append(u <= tok[:, None])
    blocks.append(u > tok[:, None])
    return np.concatenate(blocks, axis=0).astype(np.float32)


def _rec_kernel(sums_ref, q_ref, lf_ref, kk_ref, v_ref, g_ref, gw_ref, o_ref, state_ref,
                *, heads, t):
    @pl.when(pl.program_id(1) == 0)
    def _():
        state_ref[...] = jnp.zeros(state_ref.shape, F32)

    levels = _rec_levels(t)
    tok = lax.broadcasted_iota(jnp.int32, (t, HEAD_WIDTH), 0)
    row = lax.broadcasted_iota(jnp.int32, (t, t), 0)
    col = lax.broadcasted_iota(jnp.int32, (t, t), 1)
    split = row ^ col
    trans_b = (((1,), (1,)), ((), ()))
    for h in range(heads):
        cols = slice(h * HEAD_WIDTH, (h + 1) * HEAD_WIDTH)
        q, kk, v = q_ref[:, cols], kk_ref[:, cols], v_ref[:, cols]
        lf = lf_ref[:, cols]
        lf_hi = lf.astype(BF16)
        lf_lo = (lf - lf_hi.astype(F32)).astype(BF16)
        e2 = jnp.dot(sums_ref[...], jnp.concatenate([lf_hi, lf_lo], axis=1),
                     preferred_element_type=F32)
        e = e2[:, :HEAD_WIDTH] + e2[:, HEAD_WIDTH:]
        scores = jnp.where(split == 0,
                           lax.dot_general(q.astype(BF16), kk.astype(BF16), trans_b,
                                           preferred_element_type=F32), 0.0)
        for idx, w in enumerate(levels):
            dec = jnp.exp(e[idx * t:(idx + 1) * t])
            late = (tok & w) != 0
            qw = jnp.where(late, q * dec, 0.0).astype(BF16)
            kw = jnp.where(late, 0.0, kk * dec).astype(BF16)
            p = lax.dot_general(qw, kw, trans_b, preferred_element_type=F32)
            scores = scores + (p if 2 * w == t else jnp.where(split < 2 * w, p, 0.0))
        n_lv = len(levels)
        b = e[n_lv * t:(n_lv + 1) * t]
        tail = e[(n_lv + 1) * t:(n_lv + 2) * t]
        st = state_ref[h]
        o = (lax.dot_general((q * jnp.exp(b)).astype(BF16), st.astype(BF16), trans_b,
                             preferred_element_type=F32)
             + jnp.dot(scores.astype(BF16), v, preferred_element_type=F32))
        k_tail = (kk * jnp.exp(tail)).astype(BF16)
        upd = lax.dot_general(v, k_tail, (((0,), (0,)), ((), ())), preferred_element_type=F32)
        state_ref[h] = st * jnp.exp(b[t - 1:t, :]) + upd
        o = _rms(o, gw_ref[...]) * g_ref[:, cols]
        o_ref[:, cols] = o.astype(o_ref.dtype)


def _hgrn2(rq, lf, kk, ri, rg, gnorm_w, batch, seq):
    n, width = rq.shape
    heads = width // HEAD_WIDTH
    t = min(REC_BLOCK, seq)
    assert seq % t == 0 and t & (t - 1) == 0
    nb = seq // t
    sums = jnp.asarray(_rec_range_sums(t), BF16)
    blk = pl.BlockSpec((t, width), lambda b, c: (b * nb + c, 0))
    return pl.pallas_call(
        functools.partial(_rec_kernel, heads=heads, t=t),
        out_shape=jax.ShapeDtypeStruct((n, width), BF16),
        grid=(batch, nb),
        in_specs=[pl.BlockSpec(sums.shape, lambda b, c: (0, 0)), blk, blk, blk, blk, blk,
                  pl.BlockSpec((1, HEAD_WIDTH), lambda b, c: (0, 0))],
        out_specs=blk,
        scratch_shapes=[pltpu.VMEM((heads, HEAD_WIDTH, HEAD_WIDTH), F32)],
        compiler_params=_params(("parallel", "arbitrary")),
        name="hgrn2",
    )(sums, rq, lf, kk, ri, rg, gnorm_w.reshape(1, -1))


def _out_kernel(x_ref, att_ref, rec_ref, wa_ref, wr_ref, o_ref):
    o_ref[...] = (x_ref[...]
                  + jnp.dot(att_ref[...], wa_ref[...], preferred_element_type=F32)
                  + jnp.dot(rec_ref[...], wr_ref[...], preferred_element_type=F32))


def _out_proj(x, att, rec, w_out):
    n, d = x.shape
    wa = att.shape[1]
    wr = rec.shape[1]
    tm = min(OUT_TOKENS, n)
    assert n % tm == 0
    row = lambda i: (i, 0)
    return pl.pallas_call(
        _out_kernel,
        out_shape=jax.ShapeDtypeStruct((n, d), F32),
        grid=(n // tm,),
        in_specs=[pl.BlockSpec((tm, d), row), pl.BlockSpec((tm, wa), row),
                  pl.BlockSpec((tm, wr), row),
                  pl.BlockSpec((wa, d), lambda i: (0, 0)),
                  pl.BlockSpec((wr, d), lambda i: (1, 0))],
        out_specs=pl.BlockSpec((tm, d), row),
        compiler_params=_params(("parallel",)),
        name="out_proj",
    )(x, att, rec, w_out, w_out)


def _pad_hidden(w, axis, multiple):
    pad = -w.shape[axis] % multiple
    widths = [(0, 0)] * w.ndim
    widths[axis] = (0, pad)
    return jnp.pad(w, widths)


def kernel(x, ffn1_norm, ffn1_w_gate, ffn1_w_up, ffn1_w_down, mix_norm, w_in, w_out, lambda_q1, lambda_k1, lambda_q2, lambda_k2, diff_subln, hgrn_lower_bounds, hgrn_gnorm, ffn2_norm, ffn2_w_gate, ffn2_w_up, ffn2_w_down, final_norm):
    batch, seq, d = x.shape
    depth = w_in.shape[0]
    assert w_out.shape[1] == 2 * w_in.shape[2] // N_IN_PARTS

    def ffn_weights(wg, wu, wd):
        return (_pad_hidden(wg.astype(BF16), 2, FFN_HIDDEN_BLOCK),
                _pad_hidden(wu.astype(BF16), 2, FFN_HIDDEN_BLOCK),
                _pad_hidden(wd.astype(BF16), 1, FFN_HIDDEN_BLOCK))

    f1 = ffn_weights(ffn1_w_gate, ffn1_w_up, ffn1_w_down)
    f2 = ffn_weights(ffn2_w_gate, ffn2_w_up, ffn2_w_down)
    w_in_b = w_in.astype(BF16)
    w_out_b = w_out.astype(BF16)
    tables = _rope_tables(seq)
    lb_raw = hgrn_lower_bounds.astype(F32)

    h = x.reshape(batch * seq, d)
    for l in range(depth):
        lambda_init = 0.8 - 0.6 * math.exp(-0.3 * l)
        h = _ffn(h, ffn1_norm[l], f1[0][l], f1[1][l], f1[2][l])
        q, k, v, rq, lf, kk, ri, rg = _proj(h, mix_norm[l], w_in_b[l], tables, lb_raw, l, seq)
        att = _attention(q, k, v, lambda_q1[l], lambda_k1[l], lambda_q2[l], lambda_k2[l],
                         diff_subln[l], batch, seq, lambda_init)
        rec = _hgrn2(rq, lf, kk, ri, rg, hgrn_gnorm[l], batch, seq)
        h = _out_proj(h, att, rec, w_out_b[l])
        h = _ffn(h, ffn2_norm[l], f2[0][l], f2[1][l], f2[2][l],
                 final_w=final_norm if l == depth - 1 else None)
    return h.reshape(batch, seq, d)
```

```python
import functools
import math

import jax
import jax.numpy as jnp
import numpy as np
from jax import lax
from jax.experimental import pallas as pl
from jax.experimental.pallas import tpu as pltpu

NORM_EPS = 1e-6
ROPE_THETA = 10000.0
ATT_HEAD_DIM = 64
HEAD_WIDTH = 128
N_IN_PARTS = 7
LANES = 128
V7X_VMEM_LIMIT_BYTES = 60 * 1024 * 1024

FFN_TOKENS = 512
FFN_HIDDEN_BLOCK = 512
PROJ_TOKENS = 512
OUT_TOKENS = 512
ATT_Q_BLOCK = 512
ATT_KV_BLOCK = 512
ATT_SUM_ROWS = 16
ATT_STAGES = 3
REC_BLOCK = 128

F32 = jnp.float32
BF16 = jnp.bfloat16


def _rms(x, w):
    return x * lax.rsqrt(jnp.mean(x * x, axis=-1, keepdims=True) + NORM_EPS) * w


def _silu(x):
    return x * jax.nn.sigmoid(x)


def _params(semantics):
    return pltpu.CompilerParams(dimension_semantics=semantics,
                                vmem_limit_bytes=V7X_VMEM_LIMIT_BYTES)


def _ffn_kernel(x_ref, nw_ref, wg_ref, wu_ref, wd_ref, *rest, final):
    if final:
        fw_ref, o_ref, hn_ref = rest
    else:
        o_ref, hn_ref = rest
    j = pl.program_id(1)

    @pl.when(j == 0)
    def _():
        x = x_ref[...]
        hn_ref[...] = _rms(x, nw_ref[...]).astype(BF16)
        o_ref[...] = x

    h = hn_ref[...]
    g = jnp.dot(h, wg_ref[...], preferred_element_type=F32)
    u = jnp.dot(h, wu_ref[...], preferred_element_type=F32)
    a = (0.5 * _silu(g) * u).astype(BF16)
    o_ref[...] += jnp.dot(a, wd_ref[...], preferred_element_type=F32)

    if final:
        @pl.when(j == pl.num_programs(1) - 1)
        def _():
            o_ref[...] = _rms(o_ref[...], fw_ref[...])


def _ffn(x, norm_w, wg, wu, wd, final_w=None):
    n, d = x.shape
    f = wg.shape[1]
    tm = min(FFN_TOKENS, n)
    tf = FFN_HIDDEN_BLOCK
    assert n % tm == 0 and f % tf == 0
    final = final_w is not None
    row = lambda i, j: (i, 0)
    vec = pl.BlockSpec((1, d), lambda i, j: (0, 0))
    in_specs = [pl.BlockSpec((tm, d), row), vec,
                pl.BlockSpec((d, tf), lambda i, j: (0, j)),
                pl.BlockSpec((d, tf), lambda i, j: (0, j)),
                pl.BlockSpec((tf, d), lambda i, j: (j, 0))]
    args = [x, norm_w.reshape(1, d), wg, wu, wd]
    if final:
        in_specs.append(vec)
        args.append(final_w.reshape(1, d))
    return pl.pallas_call(
        functools.partial(_ffn_kernel, final=final),
        out_shape=jax.ShapeDtypeStruct((n, d), F32),
        grid=(n // tm, f // tf),
        in_specs=in_specs,
        out_specs=pl.BlockSpec((tm, d), row),
        scratch_shapes=[pltpu.VMEM((tm, d), BF16)],
        compiler_params=_params(("parallel", "arbitrary")),
        name="ffn_final" if final else "ffn",
    )(*args)


def _rope(y, cos, sin_lo, sin_hi):
    width = y.shape[1]
    from_hi = pltpu.roll(y, width - ATT_HEAD_DIM // 2, 1)
    from_lo = pltpu.roll(y, ATT_HEAD_DIM // 2, 1)
    reps = width // LANES
    tile = lambda t: jnp.concatenate([t] * reps, axis=1)
    return y * tile(cos) + from_hi * tile(sin_lo) + from_lo * tile(sin_hi)


def _proj_kernel(x_ref, nw_ref, w_ref, cos_ref, slo_ref, shi_ref, lbraw_ref,
                 q_o, k_o, v_o, rq_o, lf_o, kk_o, ri_o, rg_o, hn_ref, *, layer):
    j = pl.program_id(1)

    @pl.when(j == 0)
    def _():
        hn_ref[...] = _rms(x_ref[...], nw_ref[...]).astype(BF16)

    y = jnp.dot(hn_ref[...], w_ref[...], preferred_element_type=F32)

    @pl.when(j == 0)
    def _():
        scale = ATT_HEAD_DIM ** -0.5 * math.log2(math.e)
        q_o[...] = (_rope(y, cos_ref[...], slo_ref[...], shi_ref[...]) * scale).astype(BF16)

    @pl.when(j == 1)
    def _():
        k_o[...] = _rope(y, cos_ref[...], slo_ref[...], shi_ref[...]).astype(BF16)

    @pl.when(j == 2)
    def _():
        v_o[...] = y.T.astype(BF16)

    @pl.when(j == 3)
    def _():
        rq_o[...] = _silu(y)

    @pl.when(j == 4)
    def _():
        raw = lbraw_ref[...]
        e = jnp.exp(raw - jnp.max(raw, axis=0, keepdims=True))
        sm = e / jnp.sum(e, axis=0, keepdims=True)
        lb = jnp.sum(sm[:layer + 1], axis=0, keepdims=True) - sm[0:1]
        lf_o[...] = jnp.log(lb + (1.0 - lb) * jax.nn.sigmoid(y))
        kk_o[...] = (1.0 - lb) * jax.nn.sigmoid(-y)

    @pl.when(j == 5)
    def _():
        ri_o[...] = y.astype(BF16)

    @pl.when(j == 6)
    def _():
        rg_o[...] = _silu(y)


def _proj(x, norm_w, w_in, tables, lb_raw, layer, seq):
    n, d = x.shape
    width = w_in.shape[1] // N_IN_PARTS
    tm = min(PROJ_TOKENS, seq)
    assert n % tm == 0 and seq % tm == 0
    pos_blocks = seq // tm
    depth = lb_raw.shape[0]
    row = lambda i, j: (i, 0)
    tab = pl.BlockSpec((tm, LANES), lambda i, j: (i % pos_blocks, 0))
    out_dtypes = (BF16, BF16, BF16, F32, F32, F32, BF16, F32)
    return pl.pallas_call(
        functools.partial(_proj_kernel, layer=layer),
        out_shape=[jax.ShapeDtypeStruct((width, n) if o == 2 else (n, width), dt)
                   for o, dt in enumerate(out_dtypes)],
        grid=(n // tm, N_IN_PARTS),
        in_specs=[pl.BlockSpec((tm, d), row),
                  pl.BlockSpec((1, d), lambda i, j: (0, 0)),
                  pl.BlockSpec((d, width), lambda i, j: (0, j)),
                  tab, tab, tab,
                  pl.BlockSpec((depth, width), lambda i, j: (0, 0))],
        out_specs=[pl.BlockSpec((width, tm), lambda i, j: (0, i)) if o == 2
                   else pl.BlockSpec((tm, width), row) for o in range(len(out_dtypes))],
        scratch_shapes=[pltpu.VMEM((tm, d), BF16)],
        compiler_params=_params(("parallel", "arbitrary")),
        name="in_proj",
    )(x, norm_w.reshape(1, d), w_in, *tables, lb_raw)


def _rope_tables(seq):
    half = ATT_HEAD_DIM // 2
    inv_freq = ROPE_THETA ** (-jnp.arange(0, ATT_HEAD_DIM, 2, dtype=F32) / ATT_HEAD_DIM)
    ang = jnp.arange(seq, dtype=F32)[:, None] * inv_freq[None, :]
    cos, sin, zero = jnp.cos(ang), jnp.sin(ang), jnp.zeros_like(ang)
    reps = LANES // ATT_HEAD_DIM
    cos_t = jnp.tile(jnp.concatenate([cos, cos], axis=1), (1, reps))
    sin_lo = jnp.tile(jnp.concatenate([-sin, zero], axis=1), (1, reps))
    sin_hi = jnp.tile(jnp.concatenate([zero, sin], axis=1), (1, reps))
    del half
    return cos_t, sin_lo, sin_hi


def _att_schedule(seq, tq, tk):
    qi, kj, first, last, diag = [], [], [], [], []
    for i in range(seq // tq):
        jmax = ((i + 1) * tq - 1) // tk
        for j in range(jmax + 1):
            qi.append(i)
            kj.append(j)
            first.append(int(j == 0))
            last.append(int(j == jmax))
            diag.append(int((j + 1) * tk - 1 > i * tq))
    return [np.asarray(a, np.int32) for a in (qi, kj, first, last, diag)]


def _att_kernel(qi_ref, kj_ref, first_ref, last_ref, diag_ref,
                q_ref, k_ref, vt_ref, lq1_ref, lk1_ref, lq2_ref, lk2_ref, sw_ref,
                o_ref, qz_ref, m_ref, acc_ref, s_scr, p_scr, *, heads, tq, tk, lambda_init):
    t = pl.program_id(1)
    qi, kj = qi_ref[t], kj_ref[t]

    @pl.when(first_ref[t] == 1)
    def _():
        m_ref[...] = jnp.full(m_ref.shape, -jnp.inf, F32)
        acc_ref[...] = jnp.zeros(acc_ref.shape, F32)
        q = q_ref[...]
        low = lax.broadcasted_iota(jnp.int32, q.shape, 1) % HEAD_WIDTH < ATT_HEAD_DIM
        zero = jnp.zeros_like(q)
        qz_ref[0:tq, :] = jnp.where(low, q, zero)
        qz_ref[tq:2 * tq, :] = jnp.where(low, zero, q)

    ones_rows = jnp.ones((ATT_SUM_ROWS, tk), BF16)

    def step(masked):
        if masked:
            row = lax.broadcasted_iota(jnp.int32, (tk, 2 * tq), 0)
            col = lax.broadcasted_iota(jnp.int32, (tk, 2 * tq), 1)
            qpos = qi * tq + jnp.where(col >= tq, col - tq, col)
            keep = kj * tk + row <= qpos

        def scores(h):
            cols = slice(h * HEAD_WIDTH, (h + 1) * HEAD_WIDTH)
            s = lax.dot_general(k_ref[:, cols], qz_ref[:, cols], (((1,), (1,)), ((), ())),
                                preferred_element_type=F32)
            if masked:
                s = jnp.where(keep, s, -jnp.inf)
            s_scr[h % ATT_STAGES] = s
            return jnp.max(s, axis=0, keepdims=True)

        def probs(h, m_cur):
            m_prev = m_ref[h]
            m_new = jnp.maximum(m_prev, m_cur)
            p_scr[h % ATT_STAGES] = jnp.exp2(s_scr[h % ATT_STAGES] - m_new).astype(BF16)
            m_ref[h] = m_new
            return jnp.exp2(m_prev - m_new)

        def values(h, alpha):
            cols = slice(h * HEAD_WIDTH, (h + 1) * HEAD_WIDTH)
            vt_ext = jnp.concatenate([vt_ref[cols, :], ones_rows], axis=0)
            acc_ref[h] = alpha * acc_ref[h] + jnp.dot(vt_ext, p_scr[h % ATT_STAGES],
                                                      preferred_element_type=F32)

        ahead = ATT_STAGES - 1
        m_cur = {h: scores(h) for h in range(min(ahead, heads))}
        for h in range(heads):
            if h + ahead < heads:
                m_cur[h + ahead] = scores(h + ahead)
            values(h, probs(h, m_cur.pop(h)))

    @pl.when(diag_ref[t] == 1)
    def _():
        step(True)

    @pl.when(diag_ref[t] == 0)
    def _():
        step(False)

    @pl.when(last_ref[t] == 1)
    def _():
        lam = (jnp.exp(jnp.sum(lq1_ref[...] * lk1_ref[...], axis=1, keepdims=True))
               - jnp.exp(jnp.sum(lq2_ref[...] * lk2_ref[...], axis=1, keepdims=True))
               + lambda_init)
        for h in range(heads):
            a = acc_ref[h]
            o12 = a[0:HEAD_WIDTH] / a[HEAD_WIDTH:HEAD_WIDTH + 1]
            o = (o12[:, 0:tq] - lam * o12[:, tq:2 * tq]).T
            o = _rms(o, sw_ref[...]) * (1.0 - lambda_init)
            o_ref[:, h * HEAD_WIDTH:(h + 1) * HEAD_WIDTH] = o.astype(o_ref.dtype)


def _attention(q, k, vt, lq1, lk1, lq2, lk2, subln_w, batch, seq, lambda_init):
    n, width = q.shape
    heads = width // HEAD_WIDTH
    tq = min(ATT_Q_BLOCK, seq)
    tk = min(ATT_KV_BLOCK, seq)
    assert seq % tq == 0 and seq % tk == 0
    sched = _att_schedule(seq, tq, tk)
    nq, nk = seq // tq, seq // tk
    vec = lambda m: pl.BlockSpec((1, m), lambda b, t, *_: (0, 0))
    grid_spec = pltpu.PrefetchScalarGridSpec(
        num_scalar_prefetch=len(sched),
        grid=(batch, len(sched[0])),
        in_specs=[pl.BlockSpec((tq, width), lambda b, t, qi, kj, *_: (b * nq + qi[t], 0)),
                  pl.BlockSpec((tk, width), lambda b, t, qi, kj, *_: (b * nk + kj[t], 0)),
                  pl.BlockSpec((width, tk), lambda b, t, qi, kj, *_: (0, b * nk + kj[t])),
                  vec(ATT_HEAD_DIM), vec(ATT_HEAD_DIM), vec(ATT_HEAD_DIM), vec(ATT_HEAD_DIM),
                  vec(HEAD_WIDTH)],
        out_specs=pl.BlockSpec((tq, width), lambda b, t, qi, kj, *_: (b * nq + qi[t], 0)),
        scratch_shapes=[pltpu.VMEM((2 * tq, width), BF16),
                        pltpu.VMEM((heads, 1, 2 * tq), F32),
                        pltpu.VMEM((heads, HEAD_WIDTH + ATT_SUM_ROWS, 2 * tq), F32),
                        pltpu.VMEM((ATT_STAGES, tk, 2 * tq), F32),
                        pltpu.VMEM((ATT_STAGES, tk, 2 * tq), BF16)],
    )
    r = lambda a: a.reshape(1, -1)
    return pl.pallas_call(
        functools.partial(_att_kernel, heads=heads, tq=tq, tk=tk, lambda_init=lambda_init),
        out_shape=jax.ShapeDtypeStruct((n, width), BF16),
        grid_spec=grid_spec,
        compiler_params=_params(("parallel", "arbitrary")),
        name="diff_attention",
    )(*sched, q, k, vt, r(lq1), r(lk1), r(lq2), r(lk2), r(subln_w))


def _rec_levels(t):
    return [t >> s for s in range(1, t.bit_length())]


def _rec_range_sums(t):
    tok = np.arange(t)
    u = tok[None, :]
    blocks = []
    for w in _rec_levels(t):
        m = (tok // (2 * w)) * (2 * w) + w - 1
        late = (tok % (2 * w)) >= w
        a = np.where(late[:, None], (u > m[:, None]) & (u <= tok[:, None]),
                     (u > tok[:, None]) & (u <= m[:, None]))
        blocks.append(a)
    blocks.append(u <= tok[:, None])
    blocks.append(u > tok[:, None])
    return np.concatenate(blocks, axis=0).astype(np.float32)


def _rec_kernel(sums_ref, q_ref, lf_ref, kk_ref, v_ref, g_ref, gw_ref, o_ref, state_ref,
                *, heads, t):
    @pl.when(pl.program_id(1) == 0)
    def _():
        state_ref[...] = jnp.zeros(state_ref.shape, F32)

    levels = _rec_levels(t)
    tok = lax.broadcasted_iota(jnp.int32, (t, HEAD_WIDTH), 0)
    row = lax.broadcasted_iota(jnp.int32, (t, t), 0)
    col = lax.broadcasted_iota(jnp.int32, (t, t), 1)
    split = row ^ col
    trans_b = (((1,), (1,)), ((), ()))
    for h in range(heads):
        cols = slice(h * HEAD_WIDTH, (h + 1) * HEAD_WIDTH)
        q, kk, v = q_ref[:, cols], kk_ref[:, cols], v_ref[:, cols]
        lf = lf_ref[:, cols]
        lf_hi = lf.astype(BF16)
        lf_lo = (lf - lf_hi.astype(F32)).astype(BF16)
        e2 = jnp.dot(sums_ref[...], jnp.concatenate([lf_hi, lf_lo], axis=1),
                     preferred_element_type=F32)
        e = e2[:, :HEAD_WIDTH] + e2[:, HEAD_WIDTH:]
        scores = jnp.where(split == 0,
                           lax.dot_general(q.astype(BF16), kk.astype(BF16), trans_b,
                                           preferred_element_type=F32), 0.0)
        for idx, w in enumerate(levels):
            dec = jnp.exp(e[idx * t:(idx + 1) * t])
            late = (tok & w) != 0
            qw = jnp.where(late, q * dec, 0.0).astype(BF16)
            kw = jnp.where(late, 0.0, kk * dec).astype(BF16)
            p = lax.dot_general(qw, kw, trans_b, preferred_element_type=F32)
            scores = scores + (p if 2 * w == t else jnp.where(split < 2 * w, p, 0.0))
        n_lv = len(levels)
        b = e[n_lv * t:(n_lv + 1) * t]
        tail = e[(n_lv + 1) * t:(n_lv + 2) * t]
        st = state_ref[h]
        o = (lax.dot_general((q * jnp.exp(b)).astype(BF16), st.astype(BF16), trans_b,
                             preferred_element_type=F32)
             + jnp.dot(scores.astype(BF16), v, preferred_element_type=F32))
        k_tail = (kk * jnp.exp(tail)).astype(BF16)
        upd = lax.dot_general(v, k_tail, (((0,), (0,)), ((), ())), preferred_element_type=F32)
        state_ref[h] = st * jnp.exp(b[t - 1:t, :]) + upd
        o = _rms(o, gw_ref[...]) * g_ref[:, cols]
        o_ref[:, cols] = o.astype(o_ref.dtype)


def _hgrn2(rq, lf, kk, ri, rg, gnorm_w, batch, seq):
    n, width = rq.shape
    heads = width // HEAD_WIDTH
    t = min(REC_BLOCK, seq)
    assert seq % t == 0 and t & (t - 1) == 0
    nb = seq // t
    sums = jnp.asarray(_rec_range_sums(t), BF16)
    blk = pl.BlockSpec((t, width), lambda b, c: (b * nb + c, 0))
    return pl.pallas_call(
        functools.partial(_rec_kernel, heads=heads, t=t),
        out_shape=jax.ShapeDtypeStruct((n, width), BF16),
        grid=(batch, nb),
        in_specs=[pl.BlockSpec(sums.shape, lambda b, c: (0, 0)), blk, blk, blk, blk, blk,
                  pl.BlockSpec((1, HEAD_WIDTH), lambda b, c: (0, 0))],
        out_specs=blk,
        scratch_shapes=[pltpu.VMEM((heads, HEAD_WIDTH, HEAD_WIDTH), F32)],
        compiler_params=_params(("parallel", "arbitrary")),
        name="hgrn2",
    )(sums, rq, lf, kk, ri, rg, gnorm_w.reshape(1, -1))


def _out_kernel(x_ref, att_ref, rec_ref, wa_ref, wr_ref, o_ref):
    o_ref[...] = (x_ref[...]
                  + jnp.dot(att_ref[...], wa_ref[...], preferred_element_type=F32)
                  + jnp.dot(rec_ref[...], wr_ref[...], preferred_element_type=F32))


def _out_proj(x, att, rec, w_out):
    n, d = x.shape
    wa = att.shape[1]
    wr = rec.shape[1]
    tm = min(OUT_TOKENS, n)
    assert n % tm == 0
    row = lambda i: (i, 0)
    return pl.pallas_call(
        _out_kernel,
        out_shape=jax.ShapeDtypeStruct((n, d), F32),
        grid=(n // tm,),
        in_specs=[pl.BlockSpec((tm, d), row), pl.BlockSpec((tm, wa), row),
                  pl.BlockSpec((tm, wr), row),
                  pl.BlockSpec((wa, d), lambda i: (0, 0)),
                  pl.BlockSpec((wr, d), lambda i: (1, 0))],
        out_specs=pl.BlockSpec((tm, d), row),
        compiler_params=_params(("parallel",)),
        name="out_proj",
    )(x, att, rec, w_out, w_out)


def _pad_hidden(w, axis, multiple):
    pad = -w.shape[axis] % multiple
    widths = [(0, 0)] * w.ndim
    widths[axis] = (0, pad)
    return jnp.pad(w, widths)


def kernel(x, ffn1_norm, ffn1_w_gate, ffn1_w_up, ffn1_w_down, mix_norm, w_in, w_out, lambda_q1, lambda_k1, lambda_q2, lambda_k2, diff_subln, hgrn_lower_bounds, hgrn_gnorm, ffn2_norm, ffn2_w_gate, ffn2_w_up, ffn2_w_down, final_norm):
    batch, seq, d = x.shape
    depth = w_in.shape[0]
    assert w_out.shape[1] == 2 * w_in.shape[2] // N_IN_PARTS

    def ffn_weights(wg, wu, wd):
        return (_pad_hidden(wg.astype(BF16), 2, FFN_HIDDEN_BLOCK),
                _pad_hidden(wu.astype(BF16), 2, FFN_HIDDEN_BLOCK),
                _pad_hidden(wd.astype(BF16), 1, FFN_HIDDEN_BLOCK))

    f1 = ffn_weights(ffn1_w_gate, ffn1_w_up, ffn1_w_down)
    f2 = ffn_weights(ffn2_w_gate, ffn2_w_up, ffn2_w_down)
    w_in_b = w_in.astype(BF16)
    w_out_b = w_out.astype(BF16)
    tables = _rope_tables(seq)
    lb_raw = hgrn_lower_bounds.astype(F32)

    h = x.reshape(batch * seq, d)
    for l in range(depth):
        lambda_init = 0.8 - 0.6 * math.exp(-0.3 * l)
        h = _ffn(h, ffn1_norm[l], f1[0][l], f1[1][l], f1[2][l])
        q, k, vt, rq, lf, kk, ri, rg = _proj(h, mix_norm[l], w_in_b[l], tables, lb_raw, l, seq)
        att = _attention(q, k, vt, lambda_q1[l], lambda_k1[l], lambda_q2[l], lambda_k2[l],
                         diff_subln[l], batch, seq, lambda_init)
        rec = _hgrn2(rq, lf, kk, ri, rg, hgrn_gnorm[l], batch, seq)
        h = _out_proj(h, att, rec, w_out_b[l])
        h = _ffn(h, ffn2_norm[l], f2[0][l], f2[1][l], f2[2][l],
                 final_w=final_norm if l == depth - 1 else None)
    return h.reshape(batch, seq, d)
```

```python
import functools
import math

import jax
import jax.numpy as jnp
import numpy as np
from jax import lax
from jax.experimental import pallas as pl
from jax.experimental.pallas import tpu as pltpu

NORM_EPS = 1e-6
ROPE_THETA = 10000.0
ATT_HEAD_DIM = 64
HEAD_WIDTH = 128
N_IN_PARTS = 7
LANES = 128
V7X_VMEM_LIMIT_BYTES = 60 * 1024 * 1024

FFN_TOKENS = 512
FFN_HIDDEN_BLOCK = 1024
PROJ_TOKENS = 512
OUT_TOKENS = 512
ATT_Q_BLOCK = 512
ATT_KV_BLOCK = 512
ATT_SUM_ROWS = 16
ATT_STAGES = 3
REC_BLOCK = 128
REC_STAGES = 3

F32 = jnp.float32
BF16 = jnp.bfloat16


def _rms(x, w):
    return x * lax.rsqrt(jnp.mean(x * x, axis=-1, keepdims=True) + NORM_EPS) * w


def _silu(x):
    return x * jax.nn.sigmoid(x)


def _params(semantics):
    return pltpu.CompilerParams(dimension_semantics=semantics,
                                vmem_limit_bytes=V7X_VMEM_LIMIT_BYTES)


def _ffn_kernel(x_ref, nw_ref, wg_ref, wu_ref, wd_ref, *rest, final, tail):
    if final:
        fw_ref, o_ref, hn_ref = rest
    else:
        o_ref, hn_ref = rest
    j = pl.program_id(1)
    last = pl.num_programs(1) - 1
    tf = wg_ref.shape[1]

    @pl.when(j == 0)
    def _():
        x = x_ref[...]
        hn_ref[...] = _rms(x, nw_ref[...]).astype(BF16)
        o_ref[...] = x

    def hidden_block(width):
        h = hn_ref[...]
        g = jnp.dot(h, wg_ref[:, :width], preferred_element_type=F32)
        u = jnp.dot(h, wu_ref[:, :width], preferred_element_type=F32)
        a = (0.5 * _silu(g) * u).astype(BF16)
        o_ref[...] += jnp.dot(a, wd_ref[:width, :], preferred_element_type=F32)

    if tail == tf:
        hidden_block(tf)
    else:
        pl.when(j < last)(functools.partial(hidden_block, tf))
        pl.when(j == last)(functools.partial(hidden_block, tail))

    if final:
        @pl.when(j == last)
        def _():
            o_ref[...] = _rms(o_ref[...], fw_ref[...])


def _ffn(x, norm_w, wg, wu, wd, final_w=None):
    n, d = x.shape
    f = wg.shape[1]
    tm = min(FFN_TOKENS, n)
    tf = min(FFN_HIDDEN_BLOCK, f)
    assert n % tm == 0
    n_hidden = pl.cdiv(f, tf)
    tail = f - (n_hidden - 1) * tf
    final = final_w is not None
    row = lambda i, j: (i, 0)
    vec = pl.BlockSpec((1, d), lambda i, j: (0, 0))
    in_specs = [pl.BlockSpec((tm, d), row), vec,
                pl.BlockSpec((d, tf), lambda i, j: (0, j)),
                pl.BlockSpec((d, tf), lambda i, j: (0, j)),
                pl.BlockSpec((tf, d), lambda i, j: (j, 0))]
    args = [x, norm_w.reshape(1, d), wg, wu, wd]
    if final:
        in_specs.append(vec)
        args.append(final_w.reshape(1, d))
    return pl.pallas_call(
        functools.partial(_ffn_kernel, final=final, tail=tail),
        out_shape=jax.ShapeDtypeStruct((n, d), F32),
        grid=(n // tm, n_hidden),
        in_specs=in_specs,
        out_specs=pl.BlockSpec((tm, d), row),
        scratch_shapes=[pltpu.VMEM((tm, d), BF16)],
        compiler_params=_params(("parallel", "arbitrary")),
        name="ffn_final" if final else "ffn",
    )(*args)


def _rope(y, cos, sin_lo, sin_hi):
    width = y.shape[1]
    from_hi = pltpu.roll(y, width - ATT_HEAD_DIM // 2, 1)
    from_lo = pltpu.roll(y, ATT_HEAD_DIM // 2, 1)
    reps = width // LANES
    tile = lambda t: jnp.concatenate([t] * reps, axis=1)
    return y * tile(cos) + from_hi * tile(sin_lo) + from_lo * tile(sin_hi)


def _proj_kernel(x_ref, nw_ref, w_ref, cos_ref, slo_ref, shi_ref, lbraw_ref,
                 q_o, k_o, vt_o, rq_o, lf_o, kk_o, ri_o, rg_o, hn_ref, y_scr, *, layer):
    j = pl.program_id(1)

    @pl.when(j == 0)
    def _():
        hn_ref[...] = _rms(x_ref[...], nw_ref[...]).astype(BF16)

    def finish(part, y):
        if part == 0:
            scale = ATT_HEAD_DIM ** -0.5 * math.log2(math.e)
            q_o[...] = (_rope(y, cos_ref[...], slo_ref[...], shi_ref[...]) * scale).astype(BF16)
        elif part == 1:
            k_o[...] = _rope(y, cos_ref[...], slo_ref[...], shi_ref[...]).astype(BF16)
        elif part == 2:
            vt_o[...] = y.T.astype(BF16)
        elif part == 3:
            rq_o[...] = _silu(y)
        elif part == 4:
            raw = lbraw_ref[...]
            e = jnp.exp(raw - jnp.max(raw, axis=0, keepdims=True))
            sm = e / jnp.sum(e, axis=0, keepdims=True)
            lb = jnp.sum(sm[:layer + 1], axis=0, keepdims=True) - sm[0:1]
            lf_o[...] = jnp.log(lb + (1.0 - lb) * jax.nn.sigmoid(y))
            kk_o[...] = (1.0 - lb) * jax.nn.sigmoid(-y)
        elif part == 5:
            ri_o[...] = y.astype(BF16)
        else:
            rg_o[...] = _silu(y)

    def step(s):
        if s >= 1:
            finish(s - 1, y_scr[(s - 1) % 2])
        if s < N_IN_PARTS:
            y_scr[s % 2] = jnp.dot(hn_ref[...], w_ref[...], preferred_element_type=F32)

    for s in range(N_IN_PARTS + 1):
        pl.when(j == s)(functools.partial(step, s))


def _proj(x, norm_w, w_in, tables, lb_raw, layer, seq):
    n, d = x.shape
    width = w_in.shape[1] // N_IN_PARTS
    tm = min(PROJ_TOKENS, seq)
    assert n % tm == 0 and seq % tm == 0
    pos_blocks = seq // tm
    depth = lb_raw.shape[0]
    row = lambda i, j: (i, 0)
    tab = pl.BlockSpec((tm, LANES), lambda i, j: (i % pos_blocks, 0))
    out_dtypes = (BF16, BF16, BF16, F32, F32, F32, BF16, F32)
    return pl.pallas_call(
        functools.partial(_proj_kernel, layer=layer),
        out_shape=[jax.ShapeDtypeStruct((width, n) if o == 2 else (n, width), dt)
                   for o, dt in enumerate(out_dtypes)],
        grid=(n // tm, N_IN_PARTS + 1),
        in_specs=[pl.BlockSpec((tm, d), row),
                  pl.BlockSpec((1, d), lambda i, j: (0, 0)),
                  pl.BlockSpec((d, width), lambda i, j: (0, jnp.minimum(j, N_IN_PARTS - 1))),
                  tab, tab, tab,
                  pl.BlockSpec((depth, width), lambda i, j: (0, 0))],
        out_specs=[pl.BlockSpec((width, tm), lambda i, j: (0, i)) if o == 2
                   else pl.BlockSpec((tm, width), row) for o in range(len(out_dtypes))],
        scratch_shapes=[pltpu.VMEM((tm, d), BF16), pltpu.VMEM((2, tm, width), F32)],
        compiler_params=_params(("parallel", "arbitrary")),
        name="in_proj",
    )(x, norm_w.reshape(1, d), w_in, *tables, lb_raw)


def _rope_tables(seq):
    half = ATT_HEAD_DIM // 2
    inv_freq = ROPE_THETA ** (-jnp.arange(0, ATT_HEAD_DIM, 2, dtype=F32) / ATT_HEAD_DIM)
    ang = jnp.arange(seq, dtype=F32)[:, None] * inv_freq[None, :]
    cos, sin, zero = jnp.cos(ang), jnp.sin(ang), jnp.zeros_like(ang)
    reps = LANES // ATT_HEAD_DIM
    cos_t = jnp.tile(jnp.concatenate([cos, cos], axis=1), (1, reps))
    sin_lo = jnp.tile(jnp.concatenate([-sin, zero], axis=1), (1, reps))
    sin_hi = jnp.tile(jnp.concatenate([zero, sin], axis=1), (1, reps))
    del half
    return cos_t, sin_lo, sin_hi


def _att_schedule(seq, tq, tk):
    qi, kj, first, last, diag = [], [], [], [], []
    for i in range(seq // tq):
        jmax = ((i + 1) * tq - 1) // tk
        for j in range(jmax + 1):
            qi.append(i)
            kj.append(j)
            first.append(int(j == 0))
            last.append(int(j == jmax))
            diag.append(int((j + 1) * tk - 1 > i * tq))
    return [np.asarray(a, np.int32) for a in (qi, kj, first, last, diag)]


def _att_kernel(qi_ref, kj_ref, first_ref, last_ref, diag_ref,
                q_ref, k_ref, vt_ref, lq1_ref, lk1_ref, lq2_ref, lk2_ref, sw_ref,
                o_ref, qz_ref, m_ref, acc_ref, s_scr, p_scr, *, heads, tq, tk, lambda_init):
    t = pl.program_id(1)
    qi, kj = qi_ref[t], kj_ref[t]

    @pl.when(first_ref[t] == 1)
    def _():
        m_ref[...] = jnp.full(m_ref.shape, -jnp.inf, F32)
        acc_ref[...] = jnp.zeros(acc_ref.shape, F32)
        q = q_ref[...]
        low = lax.broadcasted_iota(jnp.int32, q.shape, 1) % HEAD_WIDTH < ATT_HEAD_DIM
        zero = jnp.zeros_like(q)
        qz_ref[0:tq, :] = jnp.where(low, q, zero)
        qz_ref[tq:2 * tq, :] = jnp.where(low, zero, q)

    ones_rows = jnp.ones((ATT_SUM_ROWS, tk), BF16)

    def step(masked):
        if masked:
            row = lax.broadcasted_iota(jnp.int32, (tk, 2 * tq), 0)
            col = lax.broadcasted_iota(jnp.int32, (tk, 2 * tq), 1)
            qpos = qi * tq + jnp.where(col >= tq, col - tq, col)
            keep = kj * tk + row <= qpos

        def scores(h):
            cols = slice(h * HEAD_WIDTH, (h + 1) * HEAD_WIDTH)
            s = lax.dot_general(k_ref[:, cols], qz_ref[:, cols], (((1,), (1,)), ((), ())),
                                preferred_element_type=F32)
            if masked:
                s = jnp.where(keep, s, -jnp.inf)
            s_scr[h % ATT_STAGES] = s
            return jnp.max(s, axis=0, keepdims=True)

        def probs(h, m_cur):
            m_prev = m_ref[h]
            m_new = jnp.maximum(m_prev, m_cur)
            p_scr[h % ATT_STAGES] = jnp.exp2(s_scr[h % ATT_STAGES] - m_new).astype(BF16)
            m_ref[h] = m_new
            return jnp.exp2(m_prev - m_new)

        def values(h, alpha):
            cols = slice(h * HEAD_WIDTH, (h + 1) * HEAD_WIDTH)
            vt_ext = jnp.concatenate([vt_ref[cols, :], ones_rows], axis=0)
            acc_ref[h] = alpha * acc_ref[h] + jnp.dot(vt_ext, p_scr[h % ATT_STAGES],
                                                      preferred_element_type=F32)

        ahead = ATT_STAGES - 1
        m_cur = {h: scores(h) for h in range(min(ahead, heads))}
        for h in range(heads):
            if h + ahead < heads:
                m_cur[h + ahead] = scores(h + ahead)
            values(h, probs(h, m_cur.pop(h)))

    @pl.when(diag_ref[t] == 1)
    def _():
        step(True)

    @pl.when(diag_ref[t] == 0)
    def _():
        step(False)

    @pl.when(last_ref[t] == 1)
    def _():
        lam = (jnp.exp(jnp.sum(lq1_ref[...] * lk1_ref[...], axis=1, keepdims=True))
               - jnp.exp(jnp.sum(lq2_ref[...] * lk2_ref[...], axis=1, keepdims=True))
               + lambda_init)
        for h in range(heads):
            a = acc_ref[h]
            o12 = a[0:HEAD_WIDTH] / a[HEAD_WIDTH:HEAD_WIDTH + 1]
            o = (o12[:, 0:tq] - lam * o12[:, tq:2 * tq]).T
            o = _rms(o, sw_ref[...]) * (1.0 - lambda_init)
            o_ref[:, h * HEAD_WIDTH:(h + 1) * HEAD_WIDTH] = o.astype(o_ref.dtype)


def _attention(q, k, vt, lq1, lk1, lq2, lk2, subln_w, batch, seq, lambda_init):
    n, width = q.shape
    heads = width // HEAD_WIDTH
    tq = min(ATT_Q_BLOCK, seq)
    tk = min(ATT_KV_BLOCK, seq)
    assert seq % tq == 0 and seq % tk == 0
    sched = _att_schedule(seq, tq, tk)
    nq, nk = seq // tq, seq // tk
    vec = lambda m: pl.BlockSpec((1, m), lambda b, t, *_: (0, 0))
    grid_spec = pltpu.PrefetchScalarGridSpec(
        num_scalar_prefetch=len(sched),
        grid=(batch, len(sched[0])),
        in_specs=[pl.BlockSpec((tq, width), lambda b, t, qi, kj, *_: (b * nq + qi[t], 0)),
                  pl.BlockSpec((tk, width), lambda b, t, qi, kj, *_: (b * nk + kj[t], 0)),
                  pl.BlockSpec((width, tk), lambda b, t, qi, kj, *_: (0, b * nk + kj[t])),
                  vec(ATT_HEAD_DIM), vec(ATT_HEAD_DIM), vec(ATT_HEAD_DIM), vec(ATT_HEAD_DIM),
                  vec(HEAD_WIDTH)],
        out_specs=pl.BlockSpec((tq, width), lambda b, t, qi, kj, *_: (b * nq + qi[t], 0)),
        scratch_shapes=[pltpu.VMEM((2 * tq, width), BF16),
                        pltpu.VMEM((heads, 1, 2 * tq), F32),
                        pltpu.VMEM((heads, HEAD_WIDTH + ATT_SUM_ROWS, 2 * tq), F32),
                        pltpu.VMEM((ATT_STAGES, tk, 2 * tq), F32),
                        pltpu.VMEM((ATT_STAGES, tk, 2 * tq), BF16)],
    )
    r = lambda a: a.reshape(1, -1)
    return pl.pallas_call(
        functools.partial(_att_kernel, heads=heads, tq=tq, tk=tk, lambda_init=lambda_init),
        out_shape=jax.ShapeDtypeStruct((n, width), BF16),
        grid_spec=grid_spec,
        compiler_params=_params(("parallel", "arbitrary")),
        name="diff_attention",
    )(*sched, q, k, vt, r(lq1), r(lk1), r(lq2), r(lk2), r(subln_w))


def _rec_levels(t):
    return [t >> s for s in range(1, t.bit_length())]


def _rec_kernel(tri_ref, q_ref, lf_ref, kk_ref, v_ref, g_ref, gw_ref, o_ref, state_ref,
                b_scr, qd_scr, kd_scr, sc_scr, *, heads, t):
    @pl.when(pl.program_id(1) == 0)
    def _():
        state_ref[...] = jnp.zeros(state_ref.shape, F32)

    levels = _rec_levels(t)
    n_lv = len(levels)
    tok = lax.broadcasted_iota(jnp.int32, (t, HEAD_WIDTH), 0)
    row = lax.broadcasted_iota(jnp.int32, (t, t), 0)
    col = lax.broadcasted_iota(jnp.int32, (t, t), 1)
    split = row ^ col
    trans_b = (((1,), (1,)), ((), ()))
    head_cols = lambda h: slice(h * HEAD_WIDTH, (h + 1) * HEAD_WIDTH)

    def cumulative_decay(h):
        lf = lf_ref[:, head_cols(h)]
        lf_hi = lf.astype(BF16)
        lf_lo = (lf - lf_hi.astype(F32)).astype(BF16)
        b2 = jnp.dot(tri_ref[...], jnp.concatenate([lf_hi, lf_lo], axis=1),
                     preferred_element_type=F32)
        b_scr[h] = b2[:, :HEAD_WIDTH] + b2[:, HEAD_WIDTH:]

    def level_exponent(h, w, lf):
        late = (tok & w) != 0
        if w == 1:
            return jnp.where(late, lf, 0.0)
        if w == 2:
            nxt = pltpu.roll(lf, t - 1, 0)
            prv = pltpu.roll(lf, 1, 0)
            odd = (tok & 1) != 0
            return jnp.where(late, jnp.where(odd, lf + prv, lf), jnp.where(odd, 0.0, nxt))
        d = jnp.concatenate([b_scr[h, r0:r0 + 2 * w] - b_scr[h, r0 + w - 1:r0 + w]
                             for r0 in range(0, t, 2 * w)], axis=0)
        return jnp.where(late, d, -d)

    def operands(h):
        q, kk, lf = q_ref[:, head_cols(h)], kk_ref[:, head_cols(h)], lf_ref[:, head_cols(h)]
        s = h % REC_STAGES
        qd_scr[s, 0] = q.astype(BF16)
        kd_scr[s, 0] = kk.astype(BF16)
        for idx, w in enumerate(levels):
            late = (tok & w) != 0
            mixed = jnp.where(late, q, kk) * jnp.exp(level_exponent(h, w, lf))
            qd_scr[s, idx + 1] = mixed.astype(BF16)
        b = b_scr[h]
        qd_scr[s, n_lv + 1] = (q * jnp.exp(b)).astype(BF16)
        kd_scr[s, 1] = (kk * jnp.exp(b_scr[h, t - 1:t] - b)).astype(BF16)

    def block_scores(h):
        s = h % REC_STAGES
        scores = lax.dot_general(qd_scr[s, 0], kd_scr[s, 0], trans_b, preferred_element_type=F32)
        for idx, w in reversed(list(enumerate(levels))):
            mixed = qd_scr[s, idx + 1]
            p = lax.dot_general(mixed, mixed, trans_b, preferred_element_type=F32)
            scores = jnp.where(split >= w, p, scores)
        sc_scr[h % 2] = jnp.where(row >= col, scores, 0.0).astype(BF16)

    def outputs(h):
        s = h % REC_STAGES
        v = v_ref[:, head_cols(h)]
        st = state_ref[h]
        o = (lax.dot_general(qd_scr[s, n_lv + 1], st.astype(BF16), trans_b,
                             preferred_element_type=F32)
             + jnp.dot(sc_scr[h % 2], v, preferred_element_type=F32))
        upd = lax.dot_general(v, kd_scr[s, 1], (((0,), (0,)), ((), ())),
                              preferred_element_type=F32)
        state_ref[h] = st * jnp.exp(b_scr[h, t - 1:t]) + upd
        o = _rms(o, gw_ref[...]) * g_ref[:, head_cols(h)]
        o_ref[:, head_cols(h)] = o.astype(o_ref.dtype)

    for h in range(heads):
        cumulative_decay(h)
    for h in range(min(2, heads)):
        operands(h)
    block_scores(0)
    for h in range(heads):
        if h + 2 < heads:
            operands(h + 2)
        if h + 1 < heads:
            block_scores(h + 1)
        outputs(h)


def _hgrn2(rq, lf, kk, ri, rg, gnorm_w, batch, seq):
    n, width = rq.shape
    heads = width // HEAD_WIDTH
    t = min(REC_BLOCK, seq)
    assert seq % t == 0 and t & (t - 1) == 0 and t >= 16
    nb = seq // t
    tri = jnp.asarray(np.tril(np.ones((t, t), np.float32)), BF16)
    n_lv = len(_rec_levels(t))
    blk = pl.BlockSpec((t, width), lambda b, c: (b * nb + c, 0))
    return pl.pallas_call(
        functools.partial(_rec_kernel, heads=heads, t=t),
        out_shape=jax.ShapeDtypeStruct((n, width), BF16),
        grid=(batch, nb),
        in_specs=[pl.BlockSpec((t, t), lambda b, c: (0, 0)), blk, blk, blk, blk, blk,
                  pl.BlockSpec((1, HEAD_WIDTH), lambda b, c: (0, 0))],
        out_specs=blk,
        scratch_shapes=[pltpu.VMEM((heads, HEAD_WIDTH, HEAD_WIDTH), F32),
                        pltpu.VMEM((heads, t, HEAD_WIDTH), F32),
                        pltpu.VMEM((REC_STAGES, n_lv + 2, t, HEAD_WIDTH), BF16),
                        pltpu.VMEM((REC_STAGES, 2, t, HEAD_WIDTH), BF16),
                        pltpu.VMEM((2, t, t), BF16)],
        compiler_params=_params(("parallel", "arbitrary")),
        name="hgrn2",
    )(tri, rq, lf, kk, ri, rg, gnorm_w.reshape(1, -1))


def _out_kernel(x_ref, att_ref, rec_ref, wa_ref, wr_ref, o_ref):
    o_ref[...] = (x_ref[...]
                  + jnp.dot(att_ref[...], wa_ref[...], preferred_element_type=F32)
                  + jnp.dot(rec_ref[...], wr_ref[...], preferred_element_type=F32))


def _out_proj(x, att, rec, w_out):
    n, d = x.shape
    wa = att.shape[1]
    wr = rec.shape[1]
    tm = min(OUT_TOKENS, n)
    assert n % tm == 0
    row = lambda i: (i, 0)
    return pl.pallas_call(
        _out_kernel,
        out_shape=jax.ShapeDtypeStruct((n, d), F32),
        grid=(n // tm,),
        in_specs=[pl.BlockSpec((tm, d), row), pl.BlockSpec((tm, wa), row),
                  pl.BlockSpec((tm, wr), row),
                  pl.BlockSpec((wa, d), lambda i: (0, 0)),
                  pl.BlockSpec((wr, d), lambda i: (1, 0))],
        out_specs=pl.BlockSpec((tm, d), row),
        compiler_params=_params(("parallel",)),
        name="out_proj",
    )(x, att, rec, w_out, w_out)


def kernel(x, ffn1_norm, ffn1_w_gate, ffn1_w_up, ffn1_w_down, mix_norm, w_in, w_out, lambda_q1, lambda_k1, lambda_q2, lambda_k2, diff_subln, hgrn_lower_bounds, hgrn_gnorm, ffn2_norm, ffn2_w_gate, ffn2_w_up, ffn2_w_down, final_norm):
    batch, seq, d = x.shape
    depth = w_in.shape[0]
    assert w_out.shape[1] == 2 * w_in.shape[2] // N_IN_PARTS

    tables = _rope_tables(seq)
    lb_raw = hgrn_lower_bounds.astype(F32)
    bf = lambda w: w.astype(BF16)

    h = x.reshape(batch * seq, d)
    for l in range(depth):
        lambda_init = 0.8 - 0.6 * math.exp(-0.3 * l)
        h = _ffn(h, ffn1_norm[l], bf(ffn1_w_gate[l]), bf(ffn1_w_up[l]), bf(ffn1_w_down[l]))
        q, k, vt, rq, lf, kk, ri, rg = _proj(h, mix_norm[l], bf(w_in[l]), tables, lb_raw, l, seq)
        att = _attention(q, k, vt, lambda_q1[l], lambda_k1[l], lambda_q2[l], lambda_k2[l],
                         diff_subln[l], batch, seq, lambda_init)
        rec = _hgrn2(rq, lf, kk, ri, rg, hgrn_gnorm[l], batch, seq)
        h = _out_proj(h, att, rec, bf(w_out[l]))
        h = _ffn(h, ffn2_norm[l], bf(ffn2_w_gate[l]), bf(ffn2_w_up[l]), bf(ffn2_w_down[l]),
                 final_w=final_norm if l == depth - 1 else None)
    return h.reshape(batch, seq, d)
```

```python
import functools
import math

import jax
import jax.numpy as jnp
import numpy as np
from jax import lax
from jax.experimental import pallas as pl
from jax.experimental.pallas import tpu as pltpu

NORM_EPS = 1e-6
ROPE_THETA = 10000.0
ATT_HEAD_DIM = 64
HEAD_WIDTH = 128
N_IN_PARTS = 7
LANES = 128
V7X_VMEM_LIMIT_BYTES = 60 * 1024 * 1024

FFN_TOKENS = 512
FFN_HIDDEN_BLOCK = 1024
PROJ_TOKENS = 512
OUT_TOKENS = 512
ATT_Q_BLOCK = 512
ATT_KV_BLOCK = 512
ATT_SUM_ROWS = 16
ATT_STAGES = 3
REC_BLOCK = 128
REC_STAGES = 3
REC_STEP_TOKENS = 512

F32 = jnp.float32
BF16 = jnp.bfloat16


def _rms(x, w):
    return x * lax.rsqrt(jnp.mean(x * x, axis=-1, keepdims=True) + NORM_EPS) * w


def _silu(x):
    return x * jax.nn.sigmoid(x)


def _params(semantics):
    return pltpu.CompilerParams(dimension_semantics=semantics,
                                vmem_limit_bytes=V7X_VMEM_LIMIT_BYTES)


def _ffn_kernel(x_ref, nw_ref, wg_ref, wu_ref, wd_ref, *rest, final, tail):
    if final:
        fw_ref, o_ref, hn_ref = rest
    else:
        o_ref, hn_ref = rest
    j = pl.program_id(1)
    last = pl.num_programs(1) - 1
    tf = wg_ref.shape[1]

    @pl.when(j == 0)
    def _():
        x = x_ref[...]
        hn_ref[...] = _rms(x, nw_ref[...]).astype(BF16)
        o_ref[...] = x

    def hidden_block(width):
        h = hn_ref[...]
        g = jnp.dot(h, wg_ref[:, :width], preferred_element_type=F32)
        u = jnp.dot(h, wu_ref[:, :width], preferred_element_type=F32)
        a = (0.5 * _silu(g) * u).astype(BF16)
        o_ref[...] += jnp.dot(a, wd_ref[:width, :], preferred_element_type=F32)

    if tail == tf:
        hidden_block(tf)
    else:
        pl.when(j == 0)(functools.partial(hidden_block, tail))
        pl.when(j > 0)(functools.partial(hidden_block, tf))

    if final:
        @pl.when(j == last)
        def _():
            o_ref[...] = _rms(o_ref[...], fw_ref[...])


def _ffn(x, norm_w, wg, wu, wd, final_w=None):
    n, d = x.shape
    f = wg.shape[1]
    tm = min(FFN_TOKENS, n)
    tf = min(FFN_HIDDEN_BLOCK, f)
    assert n % tm == 0
    n_hidden = pl.cdiv(f, tf)
    tail = f - (n_hidden - 1) * tf
    final = final_w is not None
    row = lambda i, j: (i, 0)
    vec = pl.BlockSpec((1, d), lambda i, j: (0, 0))
    hidden = (lambda j: j) if tail == tf else (lambda j: (j + n_hidden - 1) % n_hidden)
    in_specs = [pl.BlockSpec((tm, d), row), vec,
                pl.BlockSpec((d, tf), lambda i, j: (0, hidden(j))),
                pl.BlockSpec((d, tf), lambda i, j: (0, hidden(j))),
                pl.BlockSpec((tf, d), lambda i, j: (hidden(j), 0))]
    args = [x, norm_w.reshape(1, d), wg, wu, wd]
    if final:
        in_specs.append(vec)
        args.append(final_w.reshape(1, d))
    return pl.pallas_call(
        functools.partial(_ffn_kernel, final=final, tail=tail),
        out_shape=jax.ShapeDtypeStruct((n, d), F32),
        grid=(n // tm, n_hidden),
        in_specs=in_specs,
        out_specs=pl.BlockSpec((tm, d), row),
        scratch_shapes=[pltpu.VMEM((tm, d), BF16)],
        compiler_params=_params(("parallel", "arbitrary")),
        name="ffn_final" if final else "ffn",
    )(*args)


def _rope(y, cos, sin_lo, sin_hi):
    width = y.shape[1]
    from_hi = pltpu.roll(y, width - ATT_HEAD_DIM // 2, 1)
    from_lo = pltpu.roll(y, ATT_HEAD_DIM // 2, 1)
    reps = width // LANES
    tile = lambda t: jnp.concatenate([t] * reps, axis=1)
    return y * tile(cos) + from_hi * tile(sin_lo) + from_lo * tile(sin_hi)


def _proj_kernel(x_ref, nw_ref, w_ref, cos_ref, slo_ref, shi_ref, lbraw_ref,
                 q_o, k_o, vt_o, rq_o, lf_o, kk_o, ri_o, rg_o, hn_ref, y_scr, *, layer):
    j = pl.program_id(1)

    @pl.when(j == 0)
    def _():
        hn_ref[...] = _rms(x_ref[...], nw_ref[...]).astype(BF16)

    def finish(part, y):
        if part == 0:
            scale = ATT_HEAD_DIM ** -0.5 * math.log2(math.e)
            q_o[...] = (_rope(y, cos_ref[...], slo_ref[...], shi_ref[...]) * scale).astype(BF16)
        elif part == 1:
            k_o[...] = _rope(y, cos_ref[...], slo_ref[...], shi_ref[...]).astype(BF16)
        elif part == 2:
            vt_o[...] = y.T.astype(BF16)
        elif part == 3:
            rq_o[...] = _silu(y)
        elif part == 4:
            raw = lbraw_ref[...]
            e = jnp.exp(raw - jnp.max(raw, axis=0, keepdims=True))
            sm = e / jnp.sum(e, axis=0, keepdims=True)
            lb = jnp.sum(sm[:layer + 1], axis=0, keepdims=True) - sm[0:1]
            lf_o[...] = jnp.log(lb + (1.0 - lb) * jax.nn.sigmoid(y))
            kk_o[...] = (1.0 - lb) * jax.nn.sigmoid(-y)
        elif part == 5:
            ri_o[...] = y.astype(BF16)
        else:
            rg_o[...] = _silu(y)

    def step(s):
        if s >= 1:
            finish(s - 1, y_scr[(s - 1) % 2])
        y = jnp.dot(hn_ref[...], w_ref[...], preferred_element_type=F32)
        if s < N_IN_PARTS - 1:
            y_scr[s % 2] = y
        else:
            finish(s, y)

    for s in range(N_IN_PARTS):
        pl.when(j == s)(functools.partial(step, s))


def _proj(x, norm_w, w_in, tables, lb_raw, layer, seq):
    n, d = x.shape
    width = w_in.shape[1] // N_IN_PARTS
    tm = min(PROJ_TOKENS, seq)
    assert n % tm == 0 and seq % tm == 0
    pos_blocks = seq // tm
    depth = lb_raw.shape[0]
    row = lambda i, j: (i, 0)
    tab = pl.BlockSpec((tm, LANES), lambda i, j: (i % pos_blocks, 0))
    out_dtypes = (BF16, BF16, BF16, F32, F32, F32, BF16, F32)
    return pl.pallas_call(
        functools.partial(_proj_kernel, layer=layer),
        out_shape=[jax.ShapeDtypeStruct((width, n) if o == 2 else (n, width), dt)
                   for o, dt in enumerate(out_dtypes)],
        grid=(n // tm, N_IN_PARTS),
        in_specs=[pl.BlockSpec((tm, d), row),
                  pl.BlockSpec((1, d), lambda i, j: (0, 0)),
                  pl.BlockSpec((d, width), lambda i, j: (0, j)),
                  tab, tab, tab,
                  pl.BlockSpec((depth, width), lambda i, j: (0, 0))],
        out_specs=[pl.BlockSpec((width, tm), lambda i, j: (0, i)) if o == 2
                   else pl.BlockSpec((tm, width), row) for o in range(len(out_dtypes))],
        scratch_shapes=[pltpu.VMEM((tm, d), BF16), pltpu.VMEM((2, tm, width), F32)],
        compiler_params=_params(("parallel", "arbitrary")),
        name="in_proj",
    )(x, norm_w.reshape(1, d), w_in, *tables, lb_raw)


def _rope_tables(seq):
    half = ATT_HEAD_DIM // 2
    inv_freq = ROPE_THETA ** (-jnp.arange(0, ATT_HEAD_DIM, 2, dtype=F32) / ATT_HEAD_DIM)
    ang = jnp.arange(seq, dtype=F32)[:, None] * inv_freq[None, :]
    cos, sin, zero = jnp.cos(ang), jnp.sin(ang), jnp.zeros_like(ang)
    reps = LANES // ATT_HEAD_DIM
    cos_t = jnp.tile(jnp.concatenate([cos, cos], axis=1), (1, reps))
    sin_lo = jnp.tile(jnp.concatenate([-sin, zero], axis=1), (1, reps))
    sin_hi = jnp.tile(jnp.concatenate([zero, sin], axis=1), (1, reps))
    del half
    return cos_t, sin_lo, sin_hi


def _att_schedule(seq, tq, tk):
    qi, kj, first, last, diag = [], [], [], [], []
    for i in range(seq // tq):
        jmax = ((i + 1) * tq - 1) // tk
        for j in range(jmax + 1):
            qi.append(i)
            kj.append(j)
            first.append(int(j == 0))
            last.append(int(j == jmax))
            diag.append(int((j + 1) * tk - 1 > i * tq))
    return [np.asarray(a, np.int32) for a in (qi, kj, first, last, diag)]


def _att_kernel(qi_ref, kj_ref, first_ref, last_ref, diag_ref,
                q_ref, k_ref, vt_ref, lq1_ref, lk1_ref, lq2_ref, lk2_ref, sw_ref,
                o_ref, qz_ref, m_ref, acc_ref, s_scr, p_scr, *, heads, tq, tk, lambda_init):
    t = pl.program_id(1)
    qi, kj = qi_ref[t], kj_ref[t]

    @pl.when(first_ref[t] == 1)
    def _():
        m_ref[...] = jnp.full(m_ref.shape, -jnp.inf, F32)
        acc_ref[...] = jnp.zeros(acc_ref.shape, F32)
        q = q_ref[...]
        low = lax.broadcasted_iota(jnp.int32, q.shape, 1) % HEAD_WIDTH < ATT_HEAD_DIM
        zero = jnp.zeros_like(q)
        qz_ref[0:tq, :] = jnp.where(low, q, zero)
        qz_ref[tq:2 * tq, :] = jnp.where(low, zero, q)

    ones_rows = jnp.ones((ATT_SUM_ROWS, tk), BF16)

    def step(masked):
        if masked:
            row = lax.broadcasted_iota(jnp.int32, (tk, 2 * tq), 0)
            col = lax.broadcasted_iota(jnp.int32, (tk, 2 * tq), 1)
            qpos = qi * tq + jnp.where(col >= tq, col - tq, col)
            keep = kj * tk + row <= qpos

        def scores(h):
            cols = slice(h * HEAD_WIDTH, (h + 1) * HEAD_WIDTH)
            s = lax.dot_general(k_ref[:, cols], qz_ref[:, cols], (((1,), (1,)), ((), ())),
                                preferred_element_type=F32)
            if masked:
                s = jnp.where(keep, s, -jnp.inf)
            s_scr[h % ATT_STAGES] = s
            return jnp.max(s, axis=0, keepdims=True)

        def probs(h, m_cur):
            m_prev = m_ref[h]
            m_new = jnp.maximum(m_prev, m_cur)
            p_scr[h % ATT_STAGES] = jnp.exp2(s_scr[h % ATT_STAGES] - m_new).astype(BF16)
            m_ref[h] = m_new
            return jnp.exp2(m_prev - m_new)

        def values(h, alpha):
            cols = slice(h * HEAD_WIDTH, (h + 1) * HEAD_WIDTH)
            vt_ext = jnp.concatenate([vt_ref[cols, :], ones_rows], axis=0)
            acc_ref[h] = alpha * acc_ref[h] + jnp.dot(vt_ext, p_scr[h % ATT_STAGES],
                                                      preferred_element_type=F32)

        ahead = ATT_STAGES - 1
        m_cur = {h: scores(h) for h in range(min(ahead, heads))}
        for h in range(heads):
            if h + ahead < heads:
                m_cur[h + ahead] = scores(h + ahead)
            values(h, probs(h, m_cur.pop(h)))

    @pl.when(diag_ref[t] == 1)
    def _():
        step(True)

    @pl.when(diag_ref[t] == 0)
    def _():
        step(False)

    @pl.when(last_ref[t] == 1)
    def _():
        lam = (jnp.exp(jnp.sum(lq1_ref[...] * lk1_ref[...], axis=1, keepdims=True))
               - jnp.exp(jnp.sum(lq2_ref[...] * lk2_ref[...], axis=1, keepdims=True))
               + lambda_init)
        for h in range(heads):
            a = acc_ref[h]
            o12 = a[0:HEAD_WIDTH] / a[HEAD_WIDTH:HEAD_WIDTH + 1]
            o = (o12[:, 0:tq] - lam * o12[:, tq:2 * tq]).T
            o = _rms(o, sw_ref[...]) * (1.0 - lambda_init)
            o_ref[:, h * HEAD_WIDTH:(h + 1) * HEAD_WIDTH] = o.astype(o_ref.dtype)


def _attention(q, k, vt, lq1, lk1, lq2, lk2, subln_w, batch, seq, lambda_init):
    n, width = q.shape
    heads = width // HEAD_WIDTH
    tq = min(ATT_Q_BLOCK, seq)
    tk = min(ATT_KV_BLOCK, seq)
    assert seq % tq == 0 and seq % tk == 0
    sched = _att_schedule(seq, tq, tk)
    nq, nk = seq // tq, seq // tk
    vec = lambda m: pl.BlockSpec((1, m), lambda b, t, *_: (0, 0))
    grid_spec = pltpu.PrefetchScalarGridSpec(
        num_scalar_prefetch=len(sched),
        grid=(batch, len(sched[0])),
        in_specs=[pl.BlockSpec((tq, width), lambda b, t, qi, kj, *_: (b * nq + qi[t], 0)),
                  pl.BlockSpec((tk, width), lambda b, t, qi, kj, *_: (b * nk + kj[t], 0)),
                  pl.BlockSpec((width, tk), lambda b, t, qi, kj, *_: (0, b * nk + kj[t])),
                  vec(ATT_HEAD_DIM), vec(ATT_HEAD_DIM), vec(ATT_HEAD_DIM), vec(ATT_HEAD_DIM),
                  vec(HEAD_WIDTH)],
        out_specs=pl.BlockSpec((tq, width), lambda b, t, qi, kj, *_: (b * nq + qi[t], 0)),
        scratch_shapes=[pltpu.VMEM((2 * tq, width), BF16),
                        pltpu.VMEM((heads, 1, 2 * tq), F32),
                        pltpu.VMEM((heads, HEAD_WIDTH + ATT_SUM_ROWS, 2 * tq), F32),
                        pltpu.VMEM((ATT_STAGES, tk, 2 * tq), F32),
                        pltpu.VMEM((ATT_STAGES, tk, 2 * tq), BF16)],
    )
    r = lambda a: a.reshape(1, -1)
    return pl.pallas_call(
        functools.partial(_att_kernel, heads=heads, tq=tq, tk=tk, lambda_init=lambda_init),
        out_shape=jax.ShapeDtypeStruct((n, width), BF16),
        grid_spec=grid_spec,
        compiler_params=_params(("parallel", "arbitrary")),
        name="diff_attention",
    )(*sched, q, k, vt, r(lq1), r(lk1), r(lq2), r(lk2), r(subln_w))


def _rec_levels(t):
    return [t >> s for s in range(1, t.bit_length())]


def _rec_kernel(tri_ref, q_ref, lf_ref, kk_ref, v_ref, g_ref, gw_ref, o_ref, state_ref,
                b_scr, qd_scr, kd_scr, sc_scr, *, heads, t):
    @pl.when(pl.program_id(1) == 0)
    def _():
        state_ref[...] = jnp.zeros(state_ref.shape, F32)

    levels = _rec_levels(t)
    n_lv = len(levels)
    tok = lax.broadcasted_iota(jnp.int32, (t, HEAD_WIDTH), 0)
    row = lax.broadcasted_iota(jnp.int32, (t, t), 0)
    col = lax.broadcasted_iota(jnp.int32, (t, t), 1)
    split = row ^ col
    trans_b = (((1,), (1,)), ((), ()))
    head_cols = lambda h: slice(h * HEAD_WIDTH, (h + 1) * HEAD_WIDTH)

    def token_block(c, carry):
        rows = pl.ds(pl.multiple_of(c * t, t), t)

        def cumulative_decay(h):
            lf = lf_ref[rows, head_cols(h)]
            lf_hi = lf.astype(BF16)
            lf_lo = (lf - lf_hi.astype(F32)).astype(BF16)
            b2 = jnp.dot(tri_ref[...], jnp.concatenate([lf_hi, lf_lo], axis=1),
                         preferred_element_type=F32)
            b_scr[h] = b2[:, :HEAD_WIDTH] + b2[:, HEAD_WIDTH:]

        def level_exponent(h, w, lf):
            late = (tok & w) != 0
            if w == 1:
                return jnp.where(late, lf, 0.0)
            if w == 2:
                nxt = pltpu.roll(lf, t - 1, 0)
                prv = pltpu.roll(lf, 1, 0)
                odd = (tok & 1) != 0
                return jnp.where(late, jnp.where(odd, lf + prv, lf), jnp.where(odd, 0.0, nxt))
            d = jnp.concatenate([b_scr[h, r0:r0 + 2 * w] - b_scr[h, r0 + w - 1:r0 + w]
                                 for r0 in range(0, t, 2 * w)], axis=0)
            return jnp.where(late, d, -d)

        def operands(h):
            q, kk = q_ref[rows, head_cols(h)], kk_ref[rows, head_cols(h)]
            lf = lf_ref[rows, head_cols(h)]
            s = h % REC_STAGES
            qd_scr[s, 0] = q.astype(BF16)
            kd_scr[s, 0] = kk.astype(BF16)
            for idx, w in enumerate(levels):
                late = (tok & w) != 0
                mixed = jnp.where(late, q, kk) * jnp.exp(level_exponent(h, w, lf))
                qd_scr[s, idx + 1] = mixed.astype(BF16)
            b = b_scr[h]
            qd_scr[s, n_lv + 1] = (q * jnp.exp(b)).astype(BF16)
            kd_scr[s, 1] = (kk * jnp.exp(b_scr[h, t - 1:t] - b)).astype(BF16)

        def block_scores(h):
            s = h % REC_STAGES
            scores = lax.dot_general(qd_scr[s, 0], kd_scr[s, 0], trans_b,
                                     preferred_element_type=F32)
            for idx, w in reversed(list(enumerate(levels))):
                mixed = qd_scr[s, idx + 1]
                p = lax.dot_general(mixed, mixed, trans_b, preferred_element_type=F32)
                scores = jnp.where(split >= w, p, scores)
            sc_scr[h % 2] = jnp.where(row >= col, scores, 0.0).astype(BF16)

        def outputs(h):
            s = h % REC_STAGES
            v = v_ref[rows, head_cols(h)]
            st = state_ref[h]
            o = (lax.dot_general(qd_scr[s, n_lv + 1], st.astype(BF16), trans_b,
                                 preferred_element_type=F32)
                 + jnp.dot(sc_scr[h % 2], v, preferred_element_type=F32))
            upd = lax.dot_general(v, kd_scr[s, 1], (((0,), (0,)), ((), ())),
                                  preferred_element_type=F32)
            state_ref[h] = st * jnp.exp(b_scr[h, t - 1:t]) + upd
            o = _rms(o, gw_ref[...]) * g_ref[rows, head_cols(h)]
            o_ref[rows, head_cols(h)] = o.astype(o_ref.dtype)

        for h in range(heads):
            cumulative_decay(h)
        for h in range(min(2, heads)):
            operands(h)
        block_scores(0)
        for h in range(heads):
            if h + 2 < heads:
                operands(h + 2)
            if h + 1 < heads:
                block_scores(h + 1)
            outputs(h)
        return carry

    lax.fori_loop(0, q_ref.shape[0] // t, token_block, 0)


def _hgrn2(rq, lf, kk, ri, rg, gnorm_w, batch, seq):
    n, width = rq.shape
    heads = width // HEAD_WIDTH
    t = min(REC_BLOCK, seq)
    assert seq % t == 0 and t & (t - 1) == 0 and t >= 16
    ts = min(REC_STEP_TOKENS, seq)
    assert seq % ts == 0 and ts % t == 0
    nb = seq // ts
    tri =jnp.asarray(np.tril(np.ones((t, t), np.float32)), BF16)
    n_lv = len(_rec_levels(t))
    blk = pl.BlockSpec((ts, width), lambda b, c: (b * nb + c, 0))
    return pl.pallas_call(
        functools.partial(_rec_kernel, heads=heads, t=t),
        out_shape=jax.ShapeDtypeStruct((n, width), BF16),
        grid=(batch, nb),
        in_specs=[pl.BlockSpec((t, t), lambda b, c: (0, 0)), blk, blk, blk, blk, blk,
                  pl.BlockSpec((1, HEAD_WIDTH), lambda b, c: (0, 0))],
        out_specs=blk,
        scratch_shapes=[pltpu.VMEM((heads, HEAD_WIDTH, HEAD_WIDTH), F32),
                        pltpu.VMEM((heads, t, HEAD_WIDTH), F32),
                        pltpu.VMEM((REC_STAGES, n_lv + 2, t, HEAD_WIDTH), BF16),
                        pltpu.VMEM((REC_STAGES, 2, t, HEAD_WIDTH), BF16),
                        pltpu.VMEM((2, t, t), BF16)],
        compiler_params=_params(("parallel", "arbitrary")),
        name="hgrn2",
    )(tri, rq, lf, kk, ri, rg, gnorm_w.reshape(1, -1))


def _out_kernel(x_ref, att_ref, rec_ref, wa_ref, wr_ref, o_ref):
    o_ref[...] = (x_ref[...]
                  + jnp.dot(att_ref[...], wa_ref[...], preferred_element_type=F32)
                  + jnp.dot(rec_ref[...], wr_ref[...], preferred_element_type=F32))


def _out_proj(x, att, rec, w_out):
    n, d = x.shape
    wa = att.shape[1]
    wr = rec.shape[1]
    tm = min(OUT_TOKENS, n)
    assert n % tm == 0
    row = lambda i: (i, 0)
    return pl.pallas_call(
        _out_kernel,
        out_shape=jax.ShapeDtypeStruct((n, d), F32),
        grid=(n // tm,),
        in_specs=[pl.BlockSpec((tm, d), row), pl.BlockSpec((tm, wa), row),
                  pl.BlockSpec((tm, wr), row),
                  pl.BlockSpec((wa, d), lambda i: (0, 0)),
                  pl.BlockSpec((wr, d), lambda i: (1, 0))],
        out_specs=pl.BlockSpec((tm, d), row),
        compiler_params=_params(("parallel",)),
        name="out_proj",
    )(x, att, rec, w_out, w_out)


def kernel(x, ffn1_norm, ffn1_w_gate, ffn1_w_up, ffn1_w_down, mix_norm, w_in, w_out, lambda_q1, lambda_k1, lambda_q2, lambda_k2, diff_subln, hgrn_lower_bounds, hgrn_gnorm, ffn2_norm, ffn2_w_gate, ffn2_w_up, ffn2_w_down, final_norm):
    batch, seq, d = x.shape
    depth = w_in.shape[0]
    assert w_out.shape[1] == 2 * w_in.shape[2] // N_IN_PARTS

    tables = _rope_tables(seq)
    lb_raw = hgrn_lower_bounds.astype(F32)
    bf = lambda w: w.astype(BF16)

    h = x.reshape(batch * seq, d)
    for l in range(depth):
        lambda_init = 0.8 - 0.6 * math.exp(-0.3 * l)
        h = _ffn(h, ffn1_norm[l], bf(ffn1_w_gate[l]), bf(ffn1_w_up[l]), bf(ffn1_w_down[l]))
        q, k, vt, rq, lf, kk, ri, rg = _proj(h, mix_norm[l], bf(w_in[l]), tables, lb_raw, l, seq)
        att = _attention(q, k, vt, lambda_q1[l], lambda_k1[l], lambda_q2[l], lambda_k2[l],
                         diff_subln[l], batch, seq, lambda_init)
        rec = _hgrn2(rq, lf, kk, ri, rg, hgrn_gnorm[l], batch, seq)
        h = _out_proj(h, att, rec, bf(w_out[l]))
        h = _ffn(h, ffn2_norm[l], bf(ffn2_w_gate[l]), bf(ffn2_w_up[l]), bf(ffn2_w_down[l]),
                 final_w=final_norm if l == depth - 1 else None)
    return h.reshape(batch, seq, d)
```

```python
import functools
import math

import jax
import jax.numpy as jnp
import numpy as np
from jax import lax
from jax.experimental import pallas as pl
from jax.experimental.pallas import tpu as pltpu

NORM_EPS = 1e-6
ROPE_THETA = 10000.0
ATT_HEAD_DIM = 64
HEAD_WIDTH = 128
N_IN_PARTS = 7
LANES = 128
V7X_VMEM_LIMIT_BYTES = 60 * 1024 * 1024
CAST_BLOCK_BYTES = 6 * 1024 * 1024

FFN_TOKENS = 512
FFN_HIDDEN_BLOCK = 1024
PROJ_TOKENS = 512
OUT_TOKENS = 512
ATT_Q_BLOCK = 512
ATT_KV_BLOCK = 512
ATT_SUM_ROWS = 16
ATT_STAGES = 3
REC_BLOCK = 128
REC_STAGES = 3
REC_STEP_TOKENS = 512

F32 = jnp.float32
BF16 = jnp.bfloat16


def _rms(x, w):
    return x * lax.rsqrt(jnp.mean(x * x, axis=-1, keepdims=True) + NORM_EPS) * w


def _silu(x):
    return x * jax.nn.sigmoid(x)


def _params(semantics):
    return pltpu.CompilerParams(dimension_semantics=semantics,
                                vmem_limit_bytes=V7X_VMEM_LIMIT_BYTES)


def _cast_kernel(w_ref, o_ref, *, cols):
    block = w_ref.shape[1]
    w = w_ref[...]
    if cols % block:
        lane = lax.broadcasted_iota(jnp.int32, w.shape, 1)
        w = jnp.where(pl.program_id(1) * block + lane < cols, w, 0.0)
    o_ref[...] = w.astype(o_ref.dtype)


def _to_bf16_blocks(w, col_block=None):
    depth, rows, cols = w.shape
    col_block = cols if col_block is None else col_block
    n_blocks = pl.cdiv(cols, col_block)
    fits = [r for r in range(16, rows + 1, 16)
            if rows % r == 0 and r * col_block * w.dtype.itemsize <= CAST_BLOCK_BYTES]
    rb = max(fits)
    return pl.pallas_call(
        functools.partial(_cast_kernel, cols=cols),
        out_shape=jax.ShapeDtypeStruct((depth, n_blocks, rows, col_block), BF16),
        grid=(depth, n_blocks, rows // rb),
        in_specs=[pl.BlockSpec((None, rb, col_block), lambda l, c, r: (l, r, c))],
        out_specs=pl.BlockSpec((None, None, rb, col_block), lambda l, c, r: (l, c, r, 0)),
        compiler_params=_params(("parallel", "parallel", "parallel")),
        name="to_bf16",
    )(w)


def _ffn_kernel(x_ref, nw_ref, wg_ref, wu_ref, wd_ref, *rest, final, tail):
    if final:
        fw_ref, o_ref, hn_ref = rest
    else:
        o_ref, hn_ref = rest
    j = pl.program_id(1)
    last = pl.num_programs(1) - 1
    tf = wg_ref.shape[1]

    @pl.when(j == 0)
    def _():
        x = x_ref[...]
        hn_ref[...] = _rms(x, nw_ref[...]).astype(BF16)
        o_ref[...] = x

    def hidden_block(width):
        h = hn_ref[...]
        g = jnp.dot(h, wg_ref[:, :width], preferred_element_type=F32)
        u = jnp.dot(h, wu_ref[:, :width], preferred_element_type=F32)
        a = (0.5 * _silu(g) * u).astype(BF16)
        o_ref[...] += jnp.dot(a, wd_ref[:width, :], preferred_element_type=F32)

    if tail == tf:
        hidden_block(tf)
    else:
        pl.when(j == 0)(functools.partial(hidden_block, tail))
        pl.when(j > 0)(functools.partial(hidden_block, tf))

    if final:
        @pl.when(j == last)
        def _():
            o_ref[...] = _rms(o_ref[...], fw_ref[...])


def _ffn(x, norm_w, wg, wu, wd, layer, final_w=None):
    n, d = x.shape
    f = wd.shape[2]
    n_hidden, tf = wg.shape[1], wg.shape[3]
    tm = min(FFN_TOKENS, n)
    assert n % tm == 0 and n_hidden == pl.cdiv(f, tf)
    tail = f - (n_hidden - 1) * tf
    final = final_w is not None
    row = lambda i, j: (i, 0)
    vec = pl.BlockSpec((1, d), lambda i, j: (0, 0))
    hidden = (lambda j: j) if tail == tf else (lambda j: (j + n_hidden - 1) % n_hidden)
    in_specs = [pl.BlockSpec((tm, d), row), vec,
                pl.BlockSpec((None, None, d, tf), lambda i, j: (layer, hidden(j), 0, 0)),
                pl.BlockSpec((None, None, d, tf), lambda i, j: (layer, hidden(j), 0, 0)),
                pl.BlockSpec((None, None, tf, d), lambda i, j: (layer, 0, hidden(j), 0))]
    args = [x, norm_w.reshape(1, d), wg, wu, wd]
    if final:
        in_specs.append(vec)
        args.append(final_w.reshape(1, d))
    return pl.pallas_call(
        functools.partial(_ffn_kernel, final=final, tail=tail),
        out_shape=jax.ShapeDtypeStruct((n, d), F32),
        grid=(n // tm, n_hidden),
        in_specs=in_specs,
        out_specs=pl.BlockSpec((tm, d), row),
        scratch_shapes=[pltpu.VMEM((tm, d), BF16)],
        compiler_params=_params(("parallel", "arbitrary")),
        name="ffn_final" if final else "ffn",
    )(*args)


def _rope(y, cos, sin_lo, sin_hi):
    width = y.shape[1]
    from_hi = pltpu.roll(y, width - ATT_HEAD_DIM // 2, 1)
    from_lo = pltpu.roll(y, ATT_HEAD_DIM // 2, 1)
    reps = width // LANES
    tile = lambda t: jnp.concatenate([t] * reps, axis=1)
    return y * tile(cos) + from_hi * tile(sin_lo) + from_lo * tile(sin_hi)


def _proj_kernel(x_ref, nw_ref, w_ref, cos_ref, slo_ref, shi_ref, lbraw_ref,
                 q_o, k_o, vt_o, rq_o, lf_o, kk_o, ri_o, rg_o, hn_ref, y_scr, *, layer):
    j = pl.program_id(1)

    @pl.when(j == 0)
    def _():
        hn_ref[...] = _rms(x_ref[...], nw_ref[...]).astype(BF16)

    def finish(part, y):
        if part == 0:
            scale = ATT_HEAD_DIM ** -0.5 * math.log2(math.e)
            q_o[...] = (_rope(y, cos_ref[...], slo_ref[...], shi_ref[...]) * scale).astype(BF16)
        elif part == 1:
            k_o[...] = _rope(y, cos_ref[...], slo_ref[...], shi_ref[...]).astype(BF16)
        elif part == 2:
            vt_o[...] = y.T.astype(BF16)
        elif part == 3:
            rq_o[...] = _silu(y)
        elif part == 4:
            raw = lbraw_ref[...]
            e = jnp.exp(raw - jnp.max(raw, axis=0, keepdims=True))
            sm = e / jnp.sum(e, axis=0, keepdims=True)
            lb = jnp.sum(sm[:layer + 1], axis=0, keepdims=True) - sm[0:1]
            lf_o[...] = jnp.log(lb + (1.0 - lb) * jax.nn.sigmoid(y))
            kk_o[...] = (1.0 - lb) * jax.nn.sigmoid(-y)
        elif part == 5:
            ri_o[...] = y.astype(BF16)
        else:
            rg_o[...] = _silu(y)

    def step(s):
        if s >= 1:
            finish(s - 1, y_scr[(s - 1) % 2])
        y = jnp.dot(hn_ref[...], w_ref[...], preferred_element_type=F32)
        if s < N_IN_PARTS - 1:
            y_scr[s % 2] = y
        else:
            finish(s, y)

    for s in range(N_IN_PARTS):
        pl.when(j == s)(functools.partial(step, s))


def _proj(x, norm_w, w_in, tables, lb_raw, layer, seq):
    n, d = x.shape
    width = w_in.shape[3]
    assert w_in.shape[1] == N_IN_PARTS
    tm = min(PROJ_TOKENS, seq)
    assert n % tm == 0 and seq % tm == 0
    pos_blocks = seq // tm
    depth = lb_raw.shape[0]
    row = lambda i, j: (i, 0)
    tab = pl.BlockSpec((tm, LANES), lambda i, j: (i % pos_blocks, 0))
    out_dtypes = (BF16, BF16, BF16, F32, F32, F32, BF16, F32)
    return pl.pallas_call(
        functools.partial(_proj_kernel, layer=layer),
        out_shape=[jax.ShapeDtypeStruct((width, n) if o == 2 else (n, width), dt)
                   for o, dt in enumerate(out_dtypes)],
        grid=(n // tm, N_IN_PARTS),
        in_specs=[pl.BlockSpec((tm, d), row),
                  pl.BlockSpec((1, d), lambda i, j: (0, 0)),
                  pl.BlockSpec((None, None, d, width), lambda i, j: (layer, j, 0, 0)),
                  tab, tab, tab,
                  pl.BlockSpec((depth, width), lambda i, j: (0, 0))],
        out_specs=[pl.BlockSpec((width, tm), lambda i, j: (0, i)) if o == 2
                   else pl.BlockSpec((tm, width), row) for o in range(len(out_dtypes))],
        scratch_shapes=[pltpu.VMEM((tm, d), BF16), pltpu.VMEM((2, tm, width), F32)],
        compiler_params=_params(("parallel", "arbitrary")),
        name="in_proj",
    )(x, norm_w.reshape(1, d), w_in, *tables, lb_raw)


def _rope_tables(seq):
    half = ATT_HEAD_DIM // 2
    inv_freq = ROPE_THETA ** (-jnp.arange(0, ATT_HEAD_DIM, 2, dtype=F32) / ATT_HEAD_DIM)
    ang = jnp.arange(seq, dtype=F32)[:, None] * inv_freq[None, :]
    cos, sin, zero = jnp.cos(ang), jnp.sin(ang), jnp.zeros_like(ang)
    reps = LANES // ATT_HEAD_DIM
    cos_t = jnp.tile(jnp.concatenate([cos, cos], axis=1), (1, reps))
    sin_lo = jnp.tile(jnp.concatenate([-sin, zero], axis=1), (1, reps))
    sin_hi = jnp.tile(jnp.concatenate([zero, sin], axis=1), (1, reps))
    del half
    return cos_t, sin_lo, sin_hi


def _att_schedule(seq, tq, tk):
    qi, kj, first, last, diag = [], [], [], [], []
    for i in range(seq // tq):
        jmax = ((i + 1) * tq - 1) // tk
        for j in range(jmax + 1):
            qi.append(i)
            kj.append(j)
            first.append(int(j == 0))
            last.append(int(j == jmax))
            diag.append(int((j + 1) * tk - 1 > i * tq))
    return [np.asarray(a, np.int32) for a in (qi, kj, first, last, diag)]


def _att_kernel(qi_ref, kj_ref, first_ref, last_ref, diag_ref,
                q_ref, k_ref, vt_ref, lq1_ref, lk1_ref, lq2_ref, lk2_ref, sw_ref,
                o_ref, qz_ref, m_ref, acc_ref, s_scr, p_scr, *, heads, tq, tk, lambda_init):
    t = pl.program_id(1)
    qi, kj = qi_ref[t], kj_ref[t]

    @pl.when(first_ref[t] == 1)
    def _():
        m_ref[...] = jnp.full(m_ref.shape, -jnp.inf, F32)
        acc_ref[...] = jnp.zeros(acc_ref.shape, F32)
        q = q_ref[...]
        low = lax.broadcasted_iota(jnp.int32, q.shape, 1) % HEAD_WIDTH < ATT_HEAD_DIM
        zero = jnp.zeros_like(q)
        qz_ref[0:tq, :] = jnp.where(low, q, zero)
        qz_ref[tq:2 * tq, :] = jnp.where(low, zero, q)

    ones_rows = jnp.ones((ATT_SUM_ROWS, tk), BF16)

    def step(masked):
        if masked:
            row = lax.broadcasted_iota(jnp.int32, (tk, 2 * tq), 0)
            col = lax.broadcasted_iota(jnp.int32, (tk, 2 * tq), 1)
            qpos = qi * tq + jnp.where(col >= tq, col - tq, col)
            keep = kj * tk + row <= qpos

        def scores(h):
            cols = slice(h * HEAD_WIDTH, (h + 1) * HEAD_WIDTH)
            s = lax.dot_general(k_ref[:, cols], qz_ref[:, cols], (((1,), (1,)), ((), ())),
                                preferred_element_type=F32)
            if masked:
                s = jnp.where(keep, s, -jnp.inf)
            s_scr[h % ATT_STAGES] = s
            return jnp.max(s, axis=0, keepdims=True)

        def probs(h, m_cur):
            m_prev = m_ref[h]
            m_new = jnp.maximum(m_prev, m_cur)
            p_scr[h % ATT_STAGES] = jnp.exp2(s_scr[h % ATT_STAGES] - m_new).astype(BF16)
            m_ref[h] = m_new
            return jnp.exp2(m_prev - m_new)

        def values(h, alpha):
            cols = slice(h * HEAD_WIDTH, (h + 1) * HEAD_WIDTH)
            vt_ext = jnp.concatenate([vt_ref[cols, :], ones_rows], axis=0)
            acc_ref[h] = alpha * acc_ref[h] + jnp.dot(vt_ext, p_scr[h % ATT_STAGES],
                                                      preferred_element_type=F32)

        ahead = ATT_STAGES - 1
        m_cur = {h: scores(h) for h in range(min(ahead, heads))}
        for h in range(heads):
            if h + ahead < heads:
                m_cur[h + ahead] = scores(h + ahead)
            values(h, probs(h, m_cur.pop(h)))

    @pl.when(diag_ref[t] == 1)
    def _():
        step(True)

    @pl.when(diag_ref[t] == 0)
    def _():
        step(False)

    @pl.when(last_ref[t] == 1)
    def _():
        lam = (jnp.exp(jnp.sum(lq1_ref[...] * lk1_ref[...], axis=1, keepdims=True))
               - jnp.exp(jnp.sum(lq2_ref[...] * lk2_ref[...], axis=1, keepdims=True))
               + lambda_init)
        for h in range(heads):
            a = acc_ref[h]
            o12 = a[0:HEAD_WIDTH] / a[HEAD_WIDTH:HEAD_WIDTH + 1]
            o = (o12[:, 0:tq] - lam * o12[:, tq:2 * tq]).T
            o = _rms(o, sw_ref[...]) * (1.0 - lambda_init)
            o_ref[:, h * HEAD_WIDTH:(h + 1) * HEAD_WIDTH] = o.astype(o_ref.dtype)


def _attention(q, k, vt, lq1, lk1, lq2, lk2, subln_w, batch, seq, lambda_init):
    n, width = q.shape
    heads = width // HEAD_WIDTH
    tq = min(ATT_Q_BLOCK, seq)
    tk = min(ATT_KV_BLOCK, seq)
    assert seq % tq == 0 and seq % tk == 0
    sched = _att_schedule(seq, tq, tk)
    nq, nk = seq // tq, seq // tk
    vec = lambda m: pl.BlockSpec((1, m), lambda b, t, *_: (0, 0))
    grid_spec = pltpu.PrefetchScalarGridSpec(
        num_scalar_prefetch=len(sched),
        grid=(batch, len(sched[0])),
        in_specs=[pl.BlockSpec((tq, width), lambda b, t, qi, kj, *_: (b * nq + qi[t], 0)),
                  pl.BlockSpec((tk, width), lambda b, t, qi, kj, *_: (b * nk + kj[t], 0)),
                  pl.BlockSpec((width, tk), lambda b, t, qi, kj, *_: (0, b * nk + kj[t])),
                  vec(ATT_HEAD_DIM), vec(ATT_HEAD_DIM), vec(ATT_HEAD_DIM), vec(ATT_HEAD_DIM),
                  vec(HEAD_WIDTH)],
        out_specs=pl.BlockSpec((tq, width), lambda b, t, qi, kj, *_: (b * nq + qi[t], 0)),
        scratch_shapes=[pltpu.VMEM((2 * tq, width), BF16),
                        pltpu.VMEM((heads, 1, 2 * tq), F32),
                        pltpu.VMEM((heads, HEAD_WIDTH + ATT_SUM_ROWS, 2 * tq), F32),
                        pltpu.VMEM((ATT_STAGES, tk, 2 * tq), F32),
                        pltpu.VMEM((ATT_STAGES, tk, 2 * tq), BF16)],
    )
    r = lambda a: a.reshape(1, -1)
    return pl.pallas_call(
        functools.partial(_att_kernel, heads=heads, tq=tq, tk=tk, lambda_init=lambda_init),
        out_shape=jax.ShapeDtypeStruct((n, width), BF16),
        grid_spec=grid_spec,
        compiler_params=_params(("parallel", "arbitrary")),
        name="diff_attention",
    )(*sched, q, k, vt, r(lq1), r(lk1), r(lq2), r(lk2), r(subln_w))


def _rec_levels(t):
    return [t >> s for s in range(1, t.bit_length())]


def _rec_kernel(tri_ref, q_ref, lf_ref, kk_ref, v_ref, g_ref, gw_ref, o_ref, state_ref,
                b_scr, qd_scr, kd_scr, sc_scr, *, heads, t):
    @pl.when(pl.program_id(1) == 0)
    def _():
        state_ref[...] = jnp.zeros(state_ref.shape, F32)

    levels = _rec_levels(t)
    n_lv = len(levels)
    tok = lax.broadcasted_iota(jnp.int32, (t, HEAD_WIDTH), 0)
    row = lax.broadcasted_iota(jnp.int32, (t, t), 0)
    col = lax.broadcasted_iota(jnp.int32, (t, t), 1)
    split = row ^ col
    trans_b = (((1,), (1,)), ((), ()))
    head_cols = lambda h: slice(h * HEAD_WIDTH, (h + 1) * HEAD_WIDTH)

    def token_block(c, carry):
        rows = pl.ds(pl.multiple_of(c * t, t), t)

        def cumulative_decay(h):
            lf = lf_ref[rows, head_cols(h)]
            lf_hi = lf.astype(BF16)
            lf_lo = (lf - lf_hi.astype(F32)).astype(BF16)
            b2 = jnp.dot(tri_ref[...], jnp.concatenate([lf_hi, lf_lo], axis=1),
                         preferred_element_type=F32)
            b_scr[h] = b2[:, :HEAD_WIDTH] + b2[:, HEAD_WIDTH:]

        def level_exponent(h, w, lf):
            late = (tok & w) != 0
            if w == 1:
                return jnp.where(late, lf, 0.0)
            if w == 2:
                nxt = pltpu.roll(lf, t - 1, 0)
                prv = pltpu.roll(lf, 1, 0)
                odd = (tok & 1) != 0
                return jnp.where(late, jnp.where(odd, lf + prv, lf), jnp.where(odd, 0.0, nxt))
            d = jnp.concatenate([b_scr[h, r0:r0 + 2 * w] - b_scr[h, r0 + w - 1:r0 + w]
                                 for r0 in range(0, t, 2 * w)], axis=0)
            return jnp.where(late, d, -d)

        def operands(h):
            q, kk = q_ref[rows, head_cols(h)], kk_ref[rows, head_cols(h)]
            lf = lf_ref[rows, head_cols(h)]
            s = h % REC_STAGES
            qd_scr[s, 0] = q.astype(BF16)
            kd_scr[s, 0] = kk.astype(BF16)
            for idx, w in enumerate(levels):
                late = (tok & w) != 0
                mixed = jnp.where(late, q, kk) * jnp.exp(level_exponent(h, w, lf))
                qd_scr[s, idx + 1] = mixed.astype(BF16)
            b = b_scr[h]
            qd_scr[s, n_lv + 1] = (q * jnp.exp(b)).astype(BF16)
            kd_scr[s, 1] = (kk * jnp.exp(b_scr[h, t - 1:t] - b)).astype(BF16)

        def block_scores(h):
            s = h % REC_STAGES
            scores = lax.dot_general(qd_scr[s, 0], kd_scr[s, 0], trans_b,
                                     preferred_element_type=F32)
            for idx, w in reversed(list(enumerate(levels))):
                mixed = qd_scr[s, idx + 1]
                p = lax.dot_general(mixed, mixed, trans_b, preferred_element_type=F32)
                scores = jnp.where(split >= w, p, scores)
            sc_scr[h % 2] = jnp.where(row >= col, scores, 0.0).astype(BF16)

        def outputs(h):
            s = h % REC_STAGES
            v = v_ref[rows, head_cols(h)]
            st = state_ref[h]
            o = (lax.dot_general(qd_scr[s, n_lv + 1], st.astype(BF16), trans_b,
                                 preferred_element_type=F32)
                 + jnp.dot(sc_scr[h % 2], v, preferred_element_type=F32))
            upd = lax.dot_general(v, kd_scr[s, 1], (((0,), (0,)), ((), ())),
                                  preferred_element_type=F32)
            state_ref[h] = st * jnp.exp(b_scr[h, t - 1:t]) + upd
            o = _rms(o, gw_ref[...]) * g_ref[rows, head_cols(h)]
            o_ref[rows, head_cols(h)] = o.astype(o_ref.dtype)

        for h in range(heads):
            cumulative_decay(h)
        for h in range(min(2, heads)):
            operands(h)
        block_scores(0)
        for h in range(heads):
            if h + 2 < heads:
                operands(h + 2)
            if h + 1 < heads:
                block_scores(h + 1)
            outputs(h)
        return carry

    lax.fori_loop(0, q_ref.shape[0] // t, token_block, 0)


def _hgrn2(rq, lf, kk, ri, rg, gnorm_w, batch, seq):
    n, width = rq.shape
    heads = width // HEAD_WIDTH
    t = min(REC_BLOCK, seq)
    assert seq % t == 0 and t & (t - 1) == 0 and t >= 16
    ts = min(REC_STEP_TOKENS, seq)
    assert seq % ts == 0 and ts % t == 0
    nb = seq // ts
    tri =jnp.asarray(np.tril(np.ones((t, t), np.float32)), BF16)
    n_lv = len(_rec_levels(t))
    blk = pl.BlockSpec((ts, width), lambda b, c: (b * nb + c, 0))
    return pl.pallas_call(
        functools.partial(_rec_kernel, heads=heads, t=t),
        out_shape=jax.ShapeDtypeStruct((n, width), BF16),
        grid=(batch, nb),
        in_specs=[pl.BlockSpec((t, t), lambda b, c: (0, 0)), blk, blk, blk, blk, blk,
                  pl.BlockSpec((1, HEAD_WIDTH), lambda b, c: (0, 0))],
        out_specs=blk,
        scratch_shapes=[pltpu.VMEM((heads, HEAD_WIDTH, HEAD_WIDTH), F32),
                        pltpu.VMEM((heads, t, HEAD_WIDTH), F32),
                        pltpu.VMEM((REC_STAGES, n_lv + 2, t, HEAD_WIDTH), BF16),
                        pltpu.VMEM((REC_STAGES, 2, t, HEAD_WIDTH), BF16),
                        pltpu.VMEM((2, t, t), BF16)],
        compiler_params=_params(("parallel", "arbitrary")),
        name="hgrn2",
    )(tri, rq, lf, kk, ri, rg, gnorm_w.reshape(1, -1))


def _out_kernel(x_ref, att_ref, rec_ref, wa_ref, wr_ref, o_ref):
    o_ref[...] = (x_ref[...]
                  + jnp.dot(att_ref[...], wa_ref[...], preferred_element_type=F32)
                  + jnp.dot(rec_ref[...], wr_ref[...], preferred_element_type=F32))


def _out_proj(x, att, rec, w_out, layer):
    n, d = x.shape
    wa = att.shape[1]
    wr = rec.shape[1]
    tm = min(OUT_TOKENS, n)
    assert n % tm == 0
    row = lambda i: (i, 0)
    return pl.pallas_call(
        _out_kernel,
        out_shape=jax.ShapeDtypeStruct((n, d), F32),
        grid=(n // tm,),
        in_specs=[pl.BlockSpec((tm, d), row), pl.BlockSpec((tm, wa), row),
                  pl.BlockSpec((tm, wr), row),
                  pl.BlockSpec((None, None, wa, d), lambda i: (layer, 0, 0, 0)),
                  pl.BlockSpec((None, None, wr, d), lambda i: (layer, 0, 1, 0))],
        out_specs=pl.BlockSpec((tm, d), row),
        compiler_params=_params(("parallel",)),
        name="out_proj",
    )(x, att, rec, w_out, w_out)


def kernel(x, ffn1_norm, ffn1_w_gate, ffn1_w_up, ffn1_w_down, mix_norm, w_in, w_out, lambda_q1, lambda_k1, lambda_q2, lambda_k2, diff_subln, hgrn_lower_bounds, hgrn_gnorm, ffn2_norm, ffn2_w_gate, ffn2_w_up, ffn2_w_down, final_norm):
    batch, seq, d = x.shape
    depth = w_in.shape[0]
    assert w_out.shape[1] == 2 * w_in.shape[2] // N_IN_PARTS

    tables = _rope_tables(seq)
    lb_raw = hgrn_lower_bounds.astype(F32)
    tf = min(FFN_HIDDEN_BLOCK, ffn1_w_gate.shape[2])
    f1 = (_to_bf16_blocks(ffn1_w_gate, tf), _to_bf16_blocks(ffn1_w_up, tf),
          _to_bf16_blocks(ffn1_w_down))
    f2 = (_to_bf16_blocks(ffn2_w_gate, tf), _to_bf16_blocks(ffn2_w_up, tf),
          _to_bf16_blocks(ffn2_w_down))
    w_in_b = _to_bf16_blocks(w_in, w_in.shape[2] // N_IN_PARTS)
    w_out_b = _to_bf16_blocks(w_out)

    h = x.reshape(batch * seq, d)
    for l in range(depth):
        lambda_init = 0.8 - 0.6 * math.exp(-0.3 * l)
        h = _ffn(h, ffn1_norm[l], *f1, l)
        q, k, vt, rq, lf, kk, ri, rg = _proj(h, mix_norm[l], w_in_b, tables, lb_raw, l, seq)
        att = _attention(q, k, vt, lambda_q1[l], lambda_k1[l], lambda_q2[l], lambda_k2[l],
                         diff_subln[l], batch, seq, lambda_init)
        rec = _hgrn2(rq, lf, kk, ri, rg, hgrn_gnorm[l], batch, seq)
        h = _out_proj(h, att, rec, w_out_b, l)
        h = _ffn(h, ffn2_norm[l], *f2, l, final_w=final_norm if l == depth - 1 else None)
    return h.reshape(batch, seq, d)
```

```python
import functools
import math

import jax
import jax.numpy as jnp
import numpy as np
from jax import lax
from jax.experimental import pallas as pl
from jax.experimental.pallas import tpu as pltpu

NORM_EPS = 1e-6
ROPE_THETA = 10000.0
ATT_HEAD_DIM = 64
HEAD_WIDTH = 128
N_IN_PARTS = 7
LANES = 128
V7X_VMEM_LIMIT_BYTES = 60 * 1024 * 1024
CAST_BLOCK_BYTES = 6 * 1024 * 1024

FFN_TOKENS = 1024
FFN_HIDDEN_BLOCK = 512
PROJ_TOKENS = 512
OUT_TOKENS = 512
ATT_Q_BLOCK = 512
ATT_KV_BLOCK = 512
ATT_SUM_ROWS = 16
ATT_STAGES = 3
REC_BLOCK = 128
REC_STAGES = 3
REC_STEP_TOKENS = 512

F32 = jnp.float32
BF16 = jnp.bfloat16


def _rms(x, w):
    return x * lax.rsqrt(jnp.mean(x * x, axis=-1, keepdims=True) + NORM_EPS) * w


def _silu(x):
    return x * jax.nn.sigmoid(x)


def _params(semantics):
    return pltpu.CompilerParams(dimension_semantics=semantics,
                                vmem_limit_bytes=V7X_VMEM_LIMIT_BYTES)


def _cast_kernel(w_ref, o_ref, *, cols):
    block = w_ref.shape[1]
    w = w_ref[...]
    if cols % block:
        lane = lax.broadcasted_iota(jnp.int32, w.shape, 1)
        w = jnp.where(pl.program_id(1) * block + lane < cols, w, 0.0)
    o_ref[...] = w.astype(o_ref.dtype)


def _to_bf16_blocks(w, col_block=None):
    depth, rows, cols = w.shape
    col_block = cols if col_block is None else col_block
    n_blocks = pl.cdiv(cols, col_block)
    fits = [r for r in range(16, rows + 1, 16)
            if rows % r == 0 and r * col_block * w.dtype.itemsize <= CAST_BLOCK_BYTES]
    rb = max(fits)
    return pl.pallas_call(
        functools.partial(_cast_kernel, cols=cols),
        out_shape=jax.ShapeDtypeStruct((depth, n_blocks, rows, col_block), BF16),
        grid=(depth, n_blocks, rows // rb),
        in_specs=[pl.BlockSpec((None, rb, col_block), lambda l, c, r: (l, r, c))],
        out_specs=pl.BlockSpec((None, None, rb, col_block), lambda l, c, r: (l, c, r, 0)),
        compiler_params=_params(("parallel", "parallel", "parallel")),
        name="to_bf16",
    )(w)


def _ffn_kernel(x_ref, nw_ref, wg_ref, wu_ref, wd_ref, *rest, final, tail):
    if final:
        fw_ref, o_ref, hn_ref = rest
    else:
        o_ref, hn_ref = rest
    j = pl.program_id(1)
    last = pl.num_programs(1) - 1
    tf = wg_ref.shape[1]

    @pl.when(j == 0)
    def _():
        x = x_ref[...]
        hn_ref[...] = _rms(x, nw_ref[...]).astype(BF16)
        o_ref[...] = x

    def hidden_block(width):
        h = hn_ref[...]
        g = jnp.dot(h, wg_ref[:, :width], preferred_element_type=F32)
        u = jnp.dot(h, wu_ref[:, :width], preferred_element_type=F32)
        a = (0.5 * _silu(g) * u).astype(BF16)
        o_ref[...] += jnp.dot(a, wd_ref[:width, :], preferred_element_type=F32)

    if tail == tf:
        hidden_block(tf)
    else:
        pl.when(j == 0)(functools.partial(hidden_block, tail))
        pl.when(j > 0)(functools.partial(hidden_block, tf))

    if final:
        @pl.when(j == last)
        def _():
            o_ref[...] = _rms(o_ref[...], fw_ref[...])


def _ffn(x, norm_w, wg, wu, wd, layer, final_w=None):
    n, d = x.shape
    f = wd.shape[2]
    n_hidden, tf = wg.shape[1], wg.shape[3]
    tm = min(FFN_TOKENS, n)
    assert n % tm == 0 and n_hidden == pl.cdiv(f, tf)
    tail = f - (n_hidden - 1) * tf
    final = final_w is not None
    row = lambda i, j: (i, 0)
    vec = pl.BlockSpec((1, d), lambda i, j: (0, 0))
    hidden = (lambda j: j) if tail == tf else (lambda j: (j + n_hidden - 1) % n_hidden)
    in_specs = [pl.BlockSpec((tm, d), row), vec,
                pl.BlockSpec((None, None, d, tf), lambda i, j: (layer, hidden(j), 0, 0)),
                pl.BlockSpec((None, None, d, tf), lambda i, j: (layer, hidden(j), 0, 0)),
                pl.BlockSpec((None, None, tf, d), lambda i, j: (layer, 0, hidden(j), 0))]
    args = [x, norm_w.reshape(1, d), wg, wu, wd]
    if final:
        in_specs.append(vec)
        args.append(final_w.reshape(1, d))
    return pl.pallas_call(
        functools.partial(_ffn_kernel, final=final, tail=tail),
        out_shape=jax.ShapeDtypeStruct((n, d), F32),
        grid=(n // tm, n_hidden),
        in_specs=in_specs,
        out_specs=pl.BlockSpec((tm, d), row),
        scratch_shapes=[pltpu.VMEM((tm, d), BF16)],
        compiler_params=_params(("parallel", "arbitrary")),
        name="ffn_final" if final else "ffn",
    )(*args)


def _rope(y, cos, sin_lo, sin_hi):
    width = y.shape[1]
    from_hi = pltpu.roll(y, width - ATT_HEAD_DIM // 2, 1)
    from_lo = pltpu.roll(y, ATT_HEAD_DIM // 2, 1)
    reps = width // LANES
    tile = lambda t: jnp.concatenate([t] * reps, axis=1)
    return y * tile(cos) + from_hi * tile(sin_lo) + from_lo * tile(sin_hi)


def _proj_kernel(x_ref, nw_ref, w_ref, cos_ref, slo_ref, shi_ref, lbraw_ref,
                 q_o, k_o, vt_o, rq_o, lf_o, kk_o, ri_o, rg_o, hn_ref, y_scr, *, layer):
    j = pl.program_id(1)

    @pl.when(j == 0)
    def _():
        hn_ref[...] = _rms(x_ref[...], nw_ref[...]).astype(BF16)

    def finish(part, y):
        if part == 0:
            scale = ATT_HEAD_DIM ** -0.5 * math.log2(math.e)
            q_o[...] = (_rope(y, cos_ref[...], slo_ref[...], shi_ref[...]) * scale).astype(BF16)
        elif part == 1:
            k_o[...] = _rope(y, cos_ref[...], slo_ref[...], shi_ref[...]).astype(BF16)
        elif part == 2:
            vt_o[...] = y.T.astype(BF16)
        elif part == 3:
            rq_o[...] = _silu(y)
        elif part == 4:
            raw = lbraw_ref[...]
            e = jnp.exp(raw - jnp.max(raw, axis=0, keepdims=True))
            sm = e / jnp.sum(e, axis=0, keepdims=True)
            lb = jnp.sum(sm[:layer + 1], axis=0, keepdims=True) - sm[0:1]
            lf_o[...] = jnp.log(lb + (1.0 - lb) * jax.nn.sigmoid(y))
            kk_o[...] = (1.0 - lb) * jax.nn.sigmoid(-y)
        elif part == 5:
            ri_o[...] = y.astype(BF16)
        else:
            rg_o[...] = _silu(y)

    def step(s):
        if s >= 1:
            finish(s - 1, y_scr[(s - 1) % 2])
        y = jnp.dot(hn_ref[...], w_ref[...], preferred_element_type=F32)
        if s < N_IN_PARTS - 1:
            y_scr[s % 2] = y
        else:
            finish(s, y)

    for s in range(N_IN_PARTS):
        pl.when(j == s)(functools.partial(step, s))


def _proj(x, norm_w, w_in, tables, lb_raw, layer, seq):
    n, d = x.shape
    width = w_in.shape[3]
    assert w_in.shape[1] == N_IN_PARTS
    tm = min(PROJ_TOKENS, seq)
    assert n % tm == 0 and seq % tm == 0
    pos_blocks = seq // tm
    depth = lb_raw.shape[0]
    row = lambda i, j: (i, 0)
    tab = pl.BlockSpec((tm, LANES), lambda i, j: (i % pos_blocks, 0))
    out_dtypes = (BF16, BF16, BF16, F32, F32, F32, BF16, F32)
    return pl.pallas_call(
        functools.partial(_proj_kernel, layer=layer),
        out_shape=[jax.ShapeDtypeStruct((width, n) if o == 2 else (n, width), dt)
                   for o, dt in enumerate(out_dtypes)],
        grid=(n // tm, N_IN_PARTS),
        in_specs=[pl.BlockSpec((tm, d), row),
                  pl.BlockSpec((1, d), lambda i, j: (0, 0)),
                  pl.BlockSpec((None, None, d, width), lambda i, j: (layer, j, 0, 0)),
                  tab, tab, tab,
                  pl.BlockSpec((depth, width), lambda i, j: (0, 0))],
        out_specs=[pl.BlockSpec((width, tm), lambda i, j: (0, i)) if o == 2
                   else pl.BlockSpec((tm, width), row) for o in range(len(out_dtypes))],
        scratch_shapes=[pltpu.VMEM((tm, d), BF16), pltpu.VMEM((2, tm, width), F32)],
        compiler_params=_params(("parallel", "arbitrary")),
        name="in_proj",
    )(x, norm_w.reshape(1, d), w_in, *tables, lb_raw)


def _rope_tables(seq):
    half = ATT_HEAD_DIM // 2
    inv_freq = ROPE_THETA ** (-jnp.arange(0, ATT_HEAD_DIM, 2, dtype=F32) / ATT_HEAD_DIM)
    ang = jnp.arange(seq, dtype=F32)[:, None] * inv_freq[None, :]
    cos, sin, zero = jnp.cos(ang), jnp.sin(ang), jnp.zeros_like(ang)
    reps = LANES // ATT_HEAD_DIM
    cos_t = jnp.tile(jnp.concatenate([cos, cos], axis=1), (1, reps))
    sin_lo = jnp.tile(jnp.concatenate([-sin, zero], axis=1), (1, reps))
    sin_hi = jnp.tile(jnp.concatenate([zero, sin], axis=1), (1, reps))
    del half
    return cos_t, sin_lo, sin_hi


def _att_schedule(seq, tq, tk):
    qi, kj, first, last, diag = [], [], [], [], []
    for i in range(seq // tq):
        jmax = ((i + 1) * tq - 1) // tk
        for j in range(jmax + 1):
            qi.append(i)
            kj.append(j)
            first.append(int(j == 0))
            last.append(int(j == jmax))
            diag.append(int((j + 1) * tk - 1 > i * tq))
    return [np.asarray(a, np.int32) for a in (qi, kj, first, last, diag)]


def _att_kernel(qi_ref, kj_ref, first_ref, last_ref, diag_ref,
                q_ref, k_ref, vt_ref, lq1_ref, lk1_ref, lq2_ref, lk2_ref, sw_ref,
                o_ref, qz_ref, m_ref, acc_ref, s_scr, p_scr, *, heads, tq, tk, lambda_init):
    t = pl.program_id(1)
    qi, kj = qi_ref[t], kj_ref[t]

    @pl.when(first_ref[t] == 1)
    def _():
        m_ref[...] = jnp.full(m_ref.shape, -jnp.inf, F32)
        acc_ref[...] = jnp.zeros(acc_ref.shape, F32)
        q = q_ref[...]
        low = lax.broadcasted_iota(jnp.int32, q.shape, 1) % HEAD_WIDTH < ATT_HEAD_DIM
        zero = jnp.zeros_like(q)
        qz_ref[0:tq, :] = jnp.where(low, q, zero)
        qz_ref[tq:2 * tq, :] = jnp.where(low, zero, q)

    ones_rows = jnp.ones((ATT_SUM_ROWS, tk), BF16)

    def step(masked):
        if masked:
            row = lax.broadcasted_iota(jnp.int32, (tk, 2 * tq), 0)
            col = lax.broadcasted_iota(jnp.int32, (tk, 2 * tq), 1)
            qpos = qi * tq + jnp.where(col >= tq, col - tq, col)
            keep = kj * tk + row <= qpos

        def scores(h):
            cols = slice(h * HEAD_WIDTH, (h + 1) * HEAD_WIDTH)
            s = lax.dot_general(k_ref[:, cols], qz_ref[:, cols], (((1,), (1,)), ((), ())),
                                preferred_element_type=F32)
            if masked:
                s = jnp.where(keep, s, -jnp.inf)
            s_scr[h % ATT_STAGES] = s
            return jnp.max(s, axis=0, keepdims=True)

        def probs(h, m_cur):
            m_prev = m_ref[h]
            m_new = jnp.maximum(m_prev, m_cur)
            p_scr[h % ATT_STAGES] = jnp.exp2(s_scr[h % ATT_STAGES] - m_new).astype(BF16)
            m_ref[h] = m_new
            return jnp.exp2(m_prev - m_new)

        def values(h, alpha):
            cols = slice(h * HEAD_WIDTH, (h + 1) * HEAD_WIDTH)
            vt_ext = jnp.concatenate([vt_ref[cols, :], ones_rows], axis=0)
            acc_ref[h] = alpha * acc_ref[h] + jnp.dot(vt_ext, p_scr[h % ATT_STAGES],
                                                      preferred_element_type=F32)

        ahead = ATT_STAGES - 1
        m_cur = {h: scores(h) for h in range(min(ahead, heads))}
        for h in range(heads):
            if h + ahead < heads:
                m_cur[h + ahead] = scores(h + ahead)
            values(h, probs(h, m_cur.pop(h)))

    @pl.when(diag_ref[t] == 1)
    def _():
        step(True)

    @pl.when(diag_ref[t] == 0)
    def _():
        step(False)

    @pl.when(last_ref[t] == 1)
    def _():
        lam = (jnp.exp(jnp.sum(lq1_ref[...] * lk1_ref[...], axis=1, keepdims=True))
               - jnp.exp(jnp.sum(lq2_ref[...] * lk2_ref[...], axis=1, keepdims=True))
               + lambda_init)
        for h in range(heads):
            a = acc_ref[h]
            o12 = a[0:HEAD_WIDTH] / a[HEAD_WIDTH:HEAD_WIDTH + 1]
            o = (o12[:, 0:tq] - lam * o12[:, tq:2 * tq]).T
            o = _rms(o, sw_ref[...]) * (1.0 - lambda_init)
            o_ref[:, h * HEAD_WIDTH:(h + 1) * HEAD_WIDTH] = o.astype(o_ref.dtype)


def _attention(q, k, vt, lq1, lk1, lq2, lk2, subln_w, batch, seq, lambda_init):
    n, width = q.shape
    heads = width // HEAD_WIDTH
    tq = min(ATT_Q_BLOCK, seq)
    tk = min(ATT_KV_BLOCK, seq)
    assert seq % tq == 0 and seq % tk == 0
    sched = _att_schedule(seq, tq, tk)
    nq, nk = seq // tq, seq // tk
    vec = lambda m: pl.BlockSpec((1, m), lambda b, t, *_: (0, 0))
    grid_spec = pltpu.PrefetchScalarGridSpec(
        num_scalar_prefetch=len(sched),
        grid=(batch, len(sched[0])),
        in_specs=[pl.BlockSpec((tq, width), lambda b, t, qi, kj, *_: (b * nq + qi[t], 0)),
                  pl.BlockSpec((tk, width), lambda b, t, qi, kj, *_: (b * nk + kj[t], 0)),
                  pl.BlockSpec((width, tk), lambda b, t, qi, kj, *_: (0, b * nk + kj[t])),
                  vec(ATT_HEAD_DIM), vec(ATT_HEAD_DIM), vec(ATT_HEAD_DIM), vec(ATT_HEAD_DIM),
                  vec(HEAD_WIDTH)],
        out_specs=pl.BlockSpec((tq, width), lambda b, t, qi, kj, *_: (b * nq + qi[t], 0)),
        scratch_shapes=[pltpu.VMEM((2 * tq, width), BF16),
                        pltpu.VMEM((heads, 1, 2 * tq), F32),
                        pltpu.VMEM((heads, HEAD_WIDTH + ATT_SUM_ROWS, 2 * tq), F32),
                        pltpu.VMEM((ATT_STAGES, tk, 2 * tq), F32),
                        pltpu.VMEM((ATT_STAGES, tk, 2 * tq), BF16)],
    )
    r = lambda a: a.reshape(1, -1)
    return pl.pallas_call(
        functools.partial(_att_kernel, heads=heads, tq=tq, tk=tk, lambda_init=lambda_init),
        out_shape=jax.ShapeDtypeStruct((n, width), BF16),
        grid_spec=grid_spec,
        compiler_params=_params(("parallel", "arbitrary")),
        name="diff_attention",
    )(*sched, q, k, vt, r(lq1), r(lk1), r(lq2), r(lk2), r(subln_w))


def _rec_levels(t):
    return [t >> s for s in range(1, t.bit_length())]


def _rec_kernel(tri_ref, q_ref, lf_ref, kk_ref, v_ref, g_ref, gw_ref, o_ref, state_ref,
                b_scr, qd_scr, kd_scr, sc_scr, *, heads, t):
    @pl.when(pl.program_id(1) == 0)
    def _():
        state_ref[...] = jnp.zeros(state_ref.shape, F32)

    levels = _rec_levels(t)
    n_lv = len(levels)
    tok = lax.broadcasted_iota(jnp.int32, (t, HEAD_WIDTH), 0)
    row = lax.broadcasted_iota(jnp.int32, (t, t), 0)
    col = lax.broadcasted_iota(jnp.int32, (t, t), 1)
    split = row ^ col
    trans_b = (((1,), (1,)), ((), ()))
    head_cols = lambda h: slice(h * HEAD_WIDTH, (h + 1) * HEAD_WIDTH)

    def token_block(c, carry):
        rows = pl.ds(pl.multiple_of(c * t, t), t)

        def cumulative_decay(h):
            lf = lf_ref[rows, head_cols(h)]
            lf_hi = lf.astype(BF16)
            lf_lo = (lf - lf_hi.astype(F32)).astype(BF16)
            b2 = jnp.dot(tri_ref[...], jnp.concatenate([lf_hi, lf_lo], axis=1),
                         preferred_element_type=F32)
            b_scr[h] = b2[:, :HEAD_WIDTH] + b2[:, HEAD_WIDTH:]

        def level_exponent(h, w, lf):
            late = (tok & w) != 0
            if w == 1:
                return jnp.where(late, lf, 0.0)
            if w == 2:
                nxt = pltpu.roll(lf, t - 1, 0)
                prv = pltpu.roll(lf, 1, 0)
                odd = (tok & 1) != 0
                return jnp.where(late, jnp.where(odd, lf + prv, lf), jnp.where(odd, 0.0, nxt))
            d = jnp.concatenate([b_scr[h, r0:r0 + 2 * w] - b_scr[h, r0 + w - 1:r0 + w]
                                 for r0 in range(0, t, 2 * w)], axis=0)
            return jnp.where(late, d, -d)

        def operands(h):
            q, kk = q_ref[rows, head_cols(h)], kk_ref[rows, head_cols(h)]
            lf = lf_ref[rows, head_cols(h)]
            s = h % REC_STAGES
            qd_scr[s, 0] = q.astype(BF16)
            kd_scr[s, 0] = kk.astype(BF16)
            for idx, w in enumerate(levels):
                late = (tok & w) != 0
                mixed = jnp.where(late, q, kk) * jnp.exp(level_exponent(h, w, lf))
                qd_scr[s, idx + 1] = mixed.astype(BF16)
            b = b_scr[h]
            qd_scr[s, n_lv + 1] = (q * jnp.exp(b)).astype(BF16)
            kd_scr[s, 1] = (kk * jnp.exp(b_scr[h, t - 1:t] - b)).astype(BF16)

        def block_scores(h):
            s = h % REC_STAGES
            scores = lax.dot_general(qd_scr[s, 0], kd_scr[s, 0], trans_b,
                                     preferred_element_type=F32)
            for idx, w in reversed(list(enumerate(levels))):
                mixed = qd_scr[s, idx + 1]
                p = lax.dot_general(mixed, mixed, trans_b, preferred_element_type=F32)
                scores = jnp.where(split >= w, p, scores)
            sc_scr[h % 2] = jnp.where(row >= col, scores, 0.0).astype(BF16)

        def outputs(h):
            s = h % REC_STAGES
            v = v_ref[rows, head_cols(h)]
            st = state_ref[h]
            o = (lax.dot_general(qd_scr[s, n_lv + 1], st.astype(BF16), trans_b,
                                 preferred_element_type=F32)
                 + jnp.dot(sc_scr[h % 2], v, preferred_element_type=F32))
            upd = lax.dot_general(v, kd_scr[s, 1], (((0,), (0,)), ((), ())),
                                  preferred_element_type=F32)
            state_ref[h] = st * jnp.exp(b_scr[h, t - 1:t]) + upd
            o = _rms(o, gw_ref[...]) * g_ref[rows, head_cols(h)]
            o_ref[rows, head_cols(h)] = o.astype(o_ref.dtype)

        for h in range(heads):
            cumulative_decay(h)
        for h in range(min(2, heads)):
            operands(h)
        block_scores(0)
        for h in range(heads):
            if h + 2 < heads:
                operands(h + 2)
            if h + 1 < heads:
                block_scores(h + 1)
            outputs(h)
        return carry

    lax.fori_loop(0, q_ref.shape[0] // t, token_block, 0)


def _hgrn2(rq, lf, kk, ri, rg, gnorm_w, batch, seq):
    n, width = rq.shape
    heads = width // HEAD_WIDTH
    t = min(REC_BLOCK, seq)
    assert seq % t == 0 and t & (t - 1) == 0 and t >= 16
    ts = min(REC_STEP_TOKENS, seq)
    assert seq % ts == 0 and ts % t == 0
    nb = seq // ts
    tri =jnp.asarray(np.tril(np.ones((t, t), np.float32)), BF16)
    n_lv = len(_rec_levels(t))
    blk = pl.BlockSpec((ts, width), lambda b, c: (b * nb + c, 0))
    return pl.pallas_call(
        functools.partial(_rec_kernel, heads=heads, t=t),
        out_shape=jax.ShapeDtypeStruct((n, width), BF16),
        grid=(batch, nb),
        in_specs=[pl.BlockSpec((t, t), lambda b, c: (0, 0)), blk, blk, blk, blk, blk,
                  pl.BlockSpec((1, HEAD_WIDTH), lambda b, c: (0, 0))],
        out_specs=blk,
        scratch_shapes=[pltpu.VMEM((heads, HEAD_WIDTH, HEAD_WIDTH), F32),
                        pltpu.VMEM((heads, t, HEAD_WIDTH), F32),
                        pltpu.VMEM((REC_STAGES, n_lv + 2, t, HEAD_WIDTH), BF16),
                        pltpu.VMEM((REC_STAGES, 2, t, HEAD_WIDTH), BF16),
                        pltpu.VMEM((2, t, t), BF16)],
        compiler_params=_params(("parallel", "arbitrary")),
        name="hgrn2",
    )(tri, rq, lf, kk, ri, rg, gnorm_w.reshape(1, -1))


def _out_kernel(x_ref, att_ref, rec_ref, wa_ref, wr_ref, o_ref):
    o_ref[...] = (x_ref[...]
                  + jnp.dot(att_ref[...], wa_ref[...], preferred_element_type=F32)
                  + jnp.dot(rec_ref[...], wr_ref[...], preferred_element_type=F32))


def _out_proj(x, att, rec, w_out, layer):
    n, d = x.shape
    wa = att.shape[1]
    wr = rec.shape[1]
    tm = min(OUT_TOKENS, n)
    assert n % tm == 0
    row = lambda i: (i, 0)
    return pl.pallas_call(
        _out_kernel,
        out_shape=jax.ShapeDtypeStruct((n, d), F32),
        grid=(n // tm,),
        in_specs=[pl.BlockSpec((tm, d), row), pl.BlockSpec((tm, wa), row),
                  pl.BlockSpec((tm, wr), row),
                  pl.BlockSpec((None, None, wa, d), lambda i: (layer, 0, 0, 0)),
                  pl.BlockSpec((None, None, wr, d), lambda i: (layer, 0, 1, 0))],
        out_specs=pl.BlockSpec((tm, d), row),
        compiler_params=_params(("parallel",)),
        name="out_proj",
    )(x, att, rec, w_out, w_out)


def kernel(x, ffn1_norm, ffn1_w_gate, ffn1_w_up, ffn1_w_down, mix_norm, w_in, w_out, lambda_q1, lambda_k1, lambda_q2, lambda_k2, diff_subln, hgrn_lower_bounds, hgrn_gnorm, ffn2_norm, ffn2_w_gate, ffn2_w_up, ffn2_w_down, final_norm):
    batch, seq, d = x.shape
    depth = w_in.shape[0]
    assert w_out.shape[1] == 2 * w_in.shape[2] // N_IN_PARTS

    tables = _rope_tables(seq)
    lb_raw = hgrn_lower_bounds.astype(F32)
    tf = min(FFN_HIDDEN_BLOCK, ffn1_w_gate.shape[2])
    f1 = (_to_bf16_blocks(ffn1_w_gate, tf), _to_bf16_blocks(ffn1_w_up, tf),
          _to_bf16_blocks(ffn1_w_down))
    f2 = (_to_bf16_blocks(ffn2_w_gate, tf), _to_bf16_blocks(ffn2_w_up, tf),
          _to_bf16_blocks(ffn2_w_down))
    w_in_b = _to_bf16_blocks(w_in, w_in.shape[2] // N_IN_PARTS)
    w_out_b = _to_bf16_blocks(w_out)

    h = x.reshape(batch * seq, d)
    for l in range(depth):
        lambda_init = 0.8 - 0.6 * math.exp(-0.3 * l)
        h = _ffn(h, ffn1_norm[l], *f1, l)
        q, k, vt, rq, lf, kk, ri, rg = _proj(h, mix_norm[l], w_in_b, tables, lb_raw, l, seq)
        att = _attention(q, k, vt, lambda_q1[l], lambda_k1[l], lambda_q2[l], lambda_k2[l],
                         diff_subln[l], batch, seq, lambda_init)
        rec = _hgrn2(rq, lf, kk, ri, rg, hgrn_gnorm[l], batch, seq)
        h = _out_proj(h, att, rec, w_out_b, l)
        h = _ffn(h, ffn2_norm[l], *f2, l, final_w=final_norm if l == depth - 1 else None)
    return h.reshape(batch, seq, d)
```

```python
import functools
import math

import jax
import jax.numpy as jnp
import numpy as np
from jax import lax
from jax.experimental import pallas as pl
from jax.experimental.pallas import tpu as pltpu

NORM_EPS = 1e-6
ROPE_THETA = 10000.0
ATT_HEAD_DIM = 64
HEAD_WIDTH = 128
N_IN_PARTS = 7
LANES = 128
V7X_VMEM_LIMIT_BYTES = 60 * 1024 * 1024
CAST_BLOCK_BYTES = 6 * 1024 * 1024

FFN_TOKENS = 1024
FFN_HIDDEN_BLOCK = 512
PROJ_TOKENS = 512
OUT_TOKENS = 512
ATT_Q_BLOCK = 512
ATT_KV_BLOCK = 512
ATT_SUM_ROWS = 16
ATT_STAGES = 3
REC_BLOCK = 128
REC_STAGES = 3
REC_STEP_TOKENS = 512

F32 = jnp.float32
BF16 = jnp.bfloat16


def _rms(x, w):
    return x * lax.rsqrt(jnp.mean(x * x, axis=-1, keepdims=True) + NORM_EPS) * w


def _silu(x):
    return x * jax.nn.sigmoid(x)


def _read_at_step0(n_blocks):
    return lambda i, j: (jnp.where(j >= 1, jnp.minimum(i + 1, n_blocks - 1), i), 0)


def _params(semantics):
    return pltpu.CompilerParams(dimension_semantics=semantics,
                                vmem_limit_bytes=V7X_VMEM_LIMIT_BYTES)


def _cast_kernel(w_ref, o_ref, *, cols):
    block = w_ref.shape[1]
    w = w_ref[...]
    if cols % block:
        lane = lax.broadcasted_iota(jnp.int32, w.shape, 1)
        w = jnp.where(pl.program_id(1) * block + lane < cols, w, 0.0)
    o_ref[...] = w.astype(o_ref.dtype)


def _to_bf16_blocks(w, col_block=None):
    depth, rows, cols = w.shape
    col_block = cols if col_block is None else col_block
    n_blocks = pl.cdiv(cols, col_block)
    fits = [r for r in range(16, rows + 1, 16)
            if rows % r == 0 and r * col_block * w.dtype.itemsize <= CAST_BLOCK_BYTES]
    rb = max(fits)
    return pl.pallas_call(
        functools.partial(_cast_kernel, cols=cols),
        out_shape=jax.ShapeDtypeStruct((depth, n_blocks, rows, col_block), BF16),
        grid=(depth, n_blocks, rows // rb),
        in_specs=[pl.BlockSpec((None, rb, col_block), lambda l, c, r: (l, r, c))],
        out_specs=pl.BlockSpec((None, None, rb, col_block), lambda l, c, r: (l, c, r, 0)),
        compiler_params=_params(("parallel", "parallel", "parallel")),
        name="to_bf16",
    )(w)


def _ffn_kernel(x_ref, nw_ref, wg_ref, wu_ref, wd_ref, *rest, final, tail):
    if final:
        fw_ref, o_ref, hn_ref = rest
    else:
        o_ref, hn_ref = rest
    j = pl.program_id(1)
    last = pl.num_programs(1) - 1
    tf = wg_ref.shape[1]

    @pl.when(j == 0)
    def _():
        x = x_ref[...]
        hn_ref[...] = _rms(x, nw_ref[...]).astype(BF16)
        o_ref[...] = x

    def hidden_block(width):
        h = hn_ref[...]
        g = jnp.dot(h, wg_ref[:, :width], preferred_element_type=F32)
        u = jnp.dot(h, wu_ref[:, :width], preferred_element_type=F32)
        a = (0.5 * _silu(g) * u).astype(BF16)
        o_ref[...] += jnp.dot(a, wd_ref[:width, :], preferred_element_type=F32)

    if tail == tf:
        hidden_block(tf)
    else:
        pl.when(j == 0)(functools.partial(hidden_block, tail))
        pl.when(j > 0)(functools.partial(hidden_block, tf))

    if final:
        @pl.when(j == last)
        def _():
            o_ref[...] = _rms(o_ref[...], fw_ref[...])


def _ffn(x, norm_w, wg, wu, wd, layer, final_w=None):
    n, d = x.shape
    f = wd.shape[2]
    n_hidden, tf = wg.shape[1], wg.shape[3]
    tm = min(FFN_TOKENS, n)
    assert n % tm == 0 and n_hidden == pl.cdiv(f, tf)
    tail = f - (n_hidden - 1) * tf
    final = final_w is not None
    row = lambda i, j: (i, 0)
    vec = pl.BlockSpec((1, d), lambda i, j: (0, 0))
    hidden = (lambda j: j) if tail == tf else (lambda j: (j + n_hidden - 1) % n_hidden)
    in_specs = [pl.BlockSpec((tm, d), _read_at_step0(n // tm)), vec,
                pl.BlockSpec((None, None, d, tf), lambda i, j: (layer, hidden(j), 0, 0)),
                pl.BlockSpec((None, None, d, tf), lambda i, j: (layer, hidden(j), 0, 0)),
                pl.BlockSpec((None, None, tf, d), lambda i, j: (layer, 0, hidden(j), 0))]
    args = [x, norm_w.reshape(1, d), wg, wu, wd]
    if final:
        in_specs.append(vec)
        args.append(final_w.reshape(1, d))
    return pl.pallas_call(
        functools.partial(_ffn_kernel, final=final, tail=tail),
        out_shape=jax.ShapeDtypeStruct((n, d), F32),
        grid=(n // tm, n_hidden),
        in_specs=in_specs,
        out_specs=pl.BlockSpec((tm, d), row),
        scratch_shapes=[pltpu.VMEM((tm, d), BF16)],
        compiler_params=_params(("arbitrary", "arbitrary")),
        name="ffn_final" if final else "ffn",
    )(*args)


def _rope(y, cos, sin_lo, sin_hi):
    width = y.shape[1]
    from_hi = pltpu.roll(y, width - ATT_HEAD_DIM // 2, 1)
    from_lo = pltpu.roll(y, ATT_HEAD_DIM // 2, 1)
    reps = width // LANES
    tile = lambda t: jnp.concatenate([t] * reps, axis=1)
    return y * tile(cos) + from_hi * tile(sin_lo) + from_lo * tile(sin_hi)


def _proj_kernel(x_ref, nw_ref, w_ref, cos_ref, slo_ref, shi_ref, lbraw_ref,
                 q_o, k_o, vt_o, rq_o, lf_o, kk_o, ri_o, rg_o, hn_ref, y_scr, *, layer):
    j = pl.program_id(1)

    @pl.when(j == 0)
    def _():
        hn_ref[...] = _rms(x_ref[...], nw_ref[...]).astype(BF16)

    def finish(part, y):
        if part == 0:
            scale = ATT_HEAD_DIM ** -0.5 * math.log2(math.e)
            q_o[...] = (_rope(y, cos_ref[...], slo_ref[...], shi_ref[...]) * scale).astype(BF16)
        elif part == 1:
            k_o[...] = _rope(y, cos_ref[...], slo_ref[...], shi_ref[...]).astype(BF16)
        elif part == 2:
            vt_o[...] = y.T.astype(BF16)
        elif part == 3:
            rq_o[...] = _silu(y)
        elif part == 4:
            raw = lbraw_ref[...]
            e = jnp.exp(raw - jnp.max(raw, axis=0, keepdims=True))
            sm = e / jnp.sum(e, axis=0, keepdims=True)
            lb = jnp.sum(sm[:layer + 1], axis=0, keepdims=True) - sm[0:1]
            lf_o[...] = jnp.log(lb + (1.0 - lb) * jax.nn.sigmoid(y))
            kk_o[...] = (1.0 - lb) * jax.nn.sigmoid(-y)
        elif part == 5:
            ri_o[...] = y.astype(BF16)
        else:
            rg_o[...] = _silu(y)

    def step(s):
        if s >= 1:
            finish(s - 1, y_scr[(s - 1) % 2])
        y = jnp.dot(hn_ref[...], w_ref[...], preferred_element_type=F32)
        if s < N_IN_PARTS - 1:
            y_scr[s % 2] = y
        else:
            finish(s, y)

    for s in range(N_IN_PARTS):
        pl.when(j == s)(functools.partial(step, s))


def _proj(x, norm_w, w_in, tables, lb_raw, layer, seq):
    n, d = x.shape
    width = w_in.shape[3]
    assert w_in.shape[1] == N_IN_PARTS
    tm = min(PROJ_TOKENS, seq)
    assert n % tm == 0 and seq % tm == 0
    pos_blocks = seq // tm
    depth = lb_raw.shape[0]
    row = lambda i, j: (i, 0)
    tab = pl.BlockSpec((tm, LANES), lambda i, j: (i % pos_blocks, 0))
    out_dtypes = (BF16, BF16, BF16, F32, F32, F32, BF16, F32)
    out_parts = (0, 1, 2, 3, 4, 4, 5, 6)
    n_blocks = n // tm

    def out_spec(o):
        done = out_parts[o] + 1
        block = lambda i, j: jnp.where(j > done, jnp.minimum(i + 1, n_blocks - 1), i)
        if o == 2:
            return pl.BlockSpec((width, tm), lambda i, j: (0, block(i, j)))
        return pl.BlockSpec((tm, width), lambda i, j: (block(i, j), 0))

    return pl.pallas_call(
        functools.partial(_proj_kernel, layer=layer),
        out_shape=[jax.ShapeDtypeStruct((width, n) if o == 2 else (n, width), dt)
                   for o, dt in enumerate(out_dtypes)],
        grid=(n // tm, N_IN_PARTS),
        in_specs=[pl.BlockSpec((tm, d), _read_at_step0(n_blocks)),
                  pl.BlockSpec((1, d), lambda i, j: (0, 0)),
                  pl.BlockSpec((None, None, d, width), lambda i, j: (layer, j, 0, 0)),
                  tab, tab, tab,
                  pl.BlockSpec((depth, width), lambda i, j: (0, 0))],
        out_specs=[out_spec(o) for o in range(len(out_dtypes))],
        scratch_shapes=[pltpu.VMEM((tm, d), BF16), pltpu.VMEM((2, tm, width), F32)],
        compiler_params=_params(("arbitrary", "arbitrary")),
        name="in_proj",
    )(x, norm_w.reshape(1, d), w_in, *tables, lb_raw)


def _rope_tables(seq):
    half = ATT_HEAD_DIM // 2
    inv_freq = ROPE_THETA ** (-jnp.arange(0, ATT_HEAD_DIM, 2, dtype=F32) / ATT_HEAD_DIM)
    ang = jnp.arange(seq, dtype=F32)[:, None] * inv_freq[None, :]
    cos, sin, zero = jnp.cos(ang), jnp.sin(ang), jnp.zeros_like(ang)
    reps = LANES // ATT_HEAD_DIM
    cos_t = jnp.tile(jnp.concatenate([cos, cos], axis=1), (1, reps))
    sin_lo = jnp.tile(jnp.concatenate([-sin, zero], axis=1), (1, reps))
    sin_hi = jnp.tile(jnp.concatenate([zero, sin], axis=1), (1, reps))
    del half
    return cos_t, sin_lo, sin_hi


def _att_schedule(seq, tq, tk):
    qi, kj, first, last, diag = [], [], [], [], []
    for i in range(seq // tq):
        jmax = ((i + 1) * tq - 1) // tk
        for j in range(jmax + 1):
            qi.append(i)
            kj.append(j)
            first.append(int(j == 0))
            last.append(int(j == jmax))
            diag.append(int((j + 1) * tk - 1 > i * tq))
    return [np.asarray(a, np.int32) for a in (qi, kj, first, last, diag)]


def _att_kernel(qi_ref, kj_ref, first_ref, last_ref, diag_ref,
                q_ref, k_ref, vt_ref, lq1_ref, lk1_ref, lq2_ref, lk2_ref, sw_ref,
                o_ref, qz_ref, m_ref, acc_ref, s_scr, p_scr, *, heads, tq, tk, lambda_init):
    t = pl.program_id(1)
    qi, kj = qi_ref[t], kj_ref[t]

    @pl.when(first_ref[t] == 1)
    def _():
        m_ref[...] = jnp.full(m_ref.shape, -jnp.inf, F32)
        acc_ref[...] = jnp.zeros(acc_ref.shape, F32)
        q = q_ref[...]
        low = lax.broadcasted_iota(jnp.int32, q.shape, 1) % HEAD_WIDTH < ATT_HEAD_DIM
        zero = jnp.zeros_like(q)
        qz_ref[0:tq, :] = jnp.where(low, q, zero)
        qz_ref[tq:2 * tq, :] = jnp.where(low, zero, q)

    ones_rows = jnp.ones((ATT_SUM_ROWS, tk), BF16)

    def step(masked):
        if masked:
            row = lax.broadcasted_iota(jnp.int32, (tk, 2 * tq), 0)
            col = lax.broadcasted_iota(jnp.int32, (tk, 2 * tq), 1)
            qpos = qi * tq + jnp.where(col >= tq, col - tq, col)
            keep = kj * tk + row <= qpos

        def scores(h):
            cols = slice(h * HEAD_WIDTH, (h + 1) * HEAD_WIDTH)
            s = lax.dot_general(k_ref[:, cols], qz_ref[:, cols], (((1,), (1,)), ((), ())),
                                preferred_element_type=F32)
            if masked:
                s = jnp.where(keep, s, -jnp.inf)
            s_scr[h % ATT_STAGES] = s
            return jnp.max(s, axis=0, keepdims=True)

        def probs(h, m_cur):
            m_prev = m_ref[h]
            m_new = jnp.maximum(m_prev, m_cur)
            p_scr[h % ATT_STAGES] = jnp.exp2(s_scr[h % ATT_STAGES] - m_new).astype(BF16)
            m_ref[h] = m_new
            return jnp.exp2(m_prev - m_new)

        def values(h, alpha):
            cols = slice(h * HEAD_WIDTH, (h + 1) * HEAD_WIDTH)
            vt_ext = jnp.concatenate([vt_ref[cols, :], ones_rows], axis=0)
            acc_ref[h] = alpha * acc_ref[h] + jnp.dot(vt_ext, p_scr[h % ATT_STAGES],
                                                      preferred_element_type=F32)

        ahead = ATT_STAGES - 1
        m_cur = {h: scores(h) for h in range(min(ahead, heads))}
        for h in range(heads):
            if h + ahead < heads:
                m_cur[h + ahead] = scores(h + ahead)
            values(h, probs(h, m_cur.pop(h)))

    @pl.when(diag_ref[t] == 1)
    def _():
        step(True)

    @pl.when(diag_ref[t] == 0)
    def _():
        step(False)

    @pl.when(last_ref[t] == 1)
    def _():
        lam = (jnp.exp(jnp.sum(lq1_ref[...] * lk1_ref[...], axis=1, keepdims=True))
               - jnp.exp(jnp.sum(lq2_ref[...] * lk2_ref[...], axis=1, keepdims=True))
               + lambda_init)
        for h in range(heads):
            a = acc_ref[h]
            o12 = a[0:HEAD_WIDTH] / a[HEAD_WIDTH:HEAD_WIDTH + 1]
            o = (o12[:, 0:tq] - lam * o12[:, tq:2 * tq]).T
            o = _rms(o, sw_ref[...]) * (1.0 - lambda_init)
            o_ref[:, h * HEAD_WIDTH:(h + 1) * HEAD_WIDTH] = o.astype(o_ref.dtype)


def _attention(q, k, vt, lq1, lk1, lq2, lk2, subln_w, batch, seq, lambda_init):
    n, width = q.shape
    heads = width // HEAD_WIDTH
    tq = min(ATT_Q_BLOCK, seq)
    tk = min(ATT_KV_BLOCK, seq)
    assert seq % tq == 0 and seq % tk == 0
    sched = _att_schedule(seq, tq, tk)
    nq, nk = seq // tq, seq // tk
    vec = lambda m: pl.BlockSpec((1, m), lambda b, t, *_: (0, 0))
    grid_spec = pltpu.PrefetchScalarGridSpec(
        num_scalar_prefetch=len(sched),
        grid=(batch, len(sched[0])),
        in_specs=[pl.BlockSpec((tq, width), lambda b, t, qi, kj, *_: (b * nq + qi[t], 0)),
                  pl.BlockSpec((tk, width), lambda b, t, qi, kj, *_: (b * nk + kj[t], 0)),
                  pl.BlockSpec((width, tk), lambda b, t, qi, kj, *_: (0, b * nk + kj[t])),
                  vec(ATT_HEAD_DIM), vec(ATT_HEAD_DIM), vec(ATT_HEAD_DIM), vec(ATT_HEAD_DIM),
                  vec(HEAD_WIDTH)],
        out_specs=pl.BlockSpec((tq, width), lambda b, t, qi, kj, *_: (b * nq + qi[t], 0)),
        scratch_shapes=[pltpu.VMEM((2 * tq, width), BF16),
                        pltpu.VMEM((heads, 1, 2 * tq), F32),
                        pltpu.VMEM((heads, HEAD_WIDTH + ATT_SUM_ROWS, 2 * tq), F32),
                        pltpu.VMEM((ATT_STAGES, tk, 2 * tq), F32),
                        pltpu.VMEM((ATT_STAGES, tk, 2 * tq), BF16)],
    )
    r = lambda a: a.reshape(1, -1)
    return pl.pallas_call(
        functools.partial(_att_kernel, heads=heads, tq=tq, tk=tk, lambda_init=lambda_init),
        out_shape=jax.ShapeDtypeStruct((n, width), BF16),
        grid_spec=grid_spec,
        compiler_params=_params(("parallel", "arbitrary")),
        name="diff_attention",
    )(*sched, q, k, vt, r(lq1), r(lk1), r(lq2), r(lk2), r(subln_w))


def _rec_levels(t):
    return [t >> s for s in range(1, t.bit_length())]


def _rec_kernel(tri_ref, q_ref, lf_ref, kk_ref, v_ref, g_ref, gw_ref, o_ref, state_ref,
                b_scr, qd_scr, kd_scr, sc_scr, *, heads, t):
    @pl.when(pl.program_id(1) == 0)
    def _():
        state_ref[...] = jnp.zeros(state_ref.shape, F32)

    levels = _rec_levels(t)
    n_lv = len(levels)
    tok = lax.broadcasted_iota(jnp.int32, (t, HEAD_WIDTH), 0)
    row = lax.broadcasted_iota(jnp.int32, (t, t), 0)
    col = lax.broadcasted_iota(jnp.int32, (t, t), 1)
    split = row ^ col
    trans_b = (((1,), (1,)), ((), ()))
    head_cols = lambda h: slice(h * HEAD_WIDTH, (h + 1) * HEAD_WIDTH)

    def token_block(c, carry):
        rows = pl.ds(pl.multiple_of(c * t, t), t)

        def cumulative_decay(h):
            lf = lf_ref[rows, head_cols(h)]
            lf_hi = lf.astype(BF16)
            lf_lo = (lf - lf_hi.astype(F32)).astype(BF16)
            b2 = jnp.dot(tri_ref[...], jnp.concatenate([lf_hi, lf_lo], axis=1),
                         preferred_element_type=F32)
            b_scr[h] = b2[:, :HEAD_WIDTH] + b2[:, HEAD_WIDTH:]

        def level_exponent(h, w, lf):
            late = (tok & w) != 0
            if w == 1:
                return jnp.where(late, lf, 0.0)
            if w == 2:
                nxt = pltpu.roll(lf, t - 1, 0)
                prv = pltpu.roll(lf, 1, 0)
                odd = (tok & 1) != 0
                return jnp.where(late, jnp.where(odd, lf + prv, lf), jnp.where(odd, 0.0, nxt))
            d = jnp.concatenate([b_scr[h, r0:r0 + 2 * w] - b_scr[h, r0 + w - 1:r0 + w]
                                 for r0 in range(0, t, 2 * w)], axis=0)
            return jnp.where(late, d, -d)

        def operands(h):
            q, kk = q_ref[rows, head_cols(h)], kk_ref[rows, head_cols(h)]
            lf = lf_ref[rows, head_cols(h)]
            s = h % REC_STAGES
            qd_scr[s, 0] = q.astype(BF16)
            kd_scr[s, 0] = kk.astype(BF16)
            for idx, w in enumerate(levels):
                late = (tok & w) != 0
                mixed = jnp.where(late, q, kk) * jnp.exp(level_exponent(h, w, lf))
                qd_scr[s, idx + 1] = mixed.astype(BF16)
            b = b_scr[h]
            qd_scr[s, n_lv + 1] = (q * jnp.exp(b)).astype(BF16)
            kd_scr[s, 1] = (kk * jnp.exp(b_scr[h, t - 1:t] - b)).astype(BF16)

        def block_scores(h):
            s = h % REC_STAGES
            scores = lax.dot_general(qd_scr[s, 0], kd_scr[s, 0], trans_b,
                                     preferred_element_type=F32)
            for idx, w in reversed(list(enumerate(levels))):
                mixed = qd_scr[s, idx + 1]
                p = lax.dot_general(mixed, mixed, trans_b, preferred_element_type=F32)
                scores = jnp.where(split >= w, p, scores)
            sc_scr[h % 2] = jnp.where(row >= col, scores, 0.0).astype(BF16)

        def outputs(h):
            s = h % REC_STAGES
            v = v_ref[rows, head_cols(h)]
            st = state_ref[h]
            o = (lax.dot_general(qd_scr[s, n_lv + 1], st.astype(BF16), trans_b,
                                 preferred_element_type=F32)
                 + jnp.dot(sc_scr[h % 2], v, preferred_element_type=F32))
            upd = lax.dot_general(v, kd_scr[s, 1], (((0,), (0,)), ((), ())),
                                  preferred_element_type=F32)
            state_ref[h] = st * jnp.exp(b_scr[h, t - 1:t]) + upd
            o = _rms(o, gw_ref[...]) * g_ref[rows, head_cols(h)]
            o_ref[rows, head_cols(h)] = o.astype(o_ref.dtype)

        for h in range(heads):
            cumulative_decay(h)
        for h in range(min(2, heads)):
            operands(h)
        block_scores(0)
        for h in range(heads):
            if h + 2 < heads:
                operands(h + 2)
            if h + 1 < heads:
                block_scores(h + 1)
            outputs(h)
        return carry

    lax.fori_loop(0, q_ref.shape[0] // t, token_block, 0)


def _hgrn2(rq, lf, kk, ri, rg, gnorm_w, batch, seq):
    n, width = rq.shape
    heads = width // HEAD_WIDTH
    t = min(REC_BLOCK, seq)
    assert seq % t == 0 and t & (t - 1) == 0 and t >= 16
    ts = min(REC_STEP_TOKENS, seq)
    assert seq % ts == 0 and ts % t == 0
    nb = seq // ts
    tri =jnp.asarray(np.tril(np.ones((t, t), np.float32)), BF16)
    n_lv = len(_rec_levels(t))
    blk = pl.BlockSpec((ts, width), lambda b, c: (b * nb + c, 0))
    return pl.pallas_call(
        functools.partial(_rec_kernel, heads=heads, t=t),
        out_shape=jax.ShapeDtypeStruct((n, width), BF16),
        grid=(batch, nb),
        in_specs=[pl.BlockSpec((t, t), lambda b, c: (0, 0)), blk, blk, blk, blk, blk,
                  pl.BlockSpec((1, HEAD_WIDTH), lambda b, c: (0, 0))],
        out_specs=blk,
        scratch_shapes=[pltpu.VMEM((heads, HEAD_WIDTH, HEAD_WIDTH), F32),
                        pltpu.VMEM((heads, t, HEAD_WIDTH), F32),
                        pltpu.VMEM((REC_STAGES, n_lv + 2, t, HEAD_WIDTH), BF16),
                        pltpu.VMEM((REC_STAGES, 2, t, HEAD_WIDTH), BF16),
                        pltpu.VMEM((2, t, t), BF16)],
        compiler_params=_params(("parallel", "arbitrary")),
        name="hgrn2",
    )(tri, rq, lf, kk, ri, rg, gnorm_w.reshape(1, -1))


def _out_kernel(x_ref, att_ref, rec_ref, wa_ref, wr_ref, o_ref):
    o_ref[...] = (x_ref[...]
                  + jnp.dot(att_ref[...], wa_ref[...], preferred_element_type=F32)
                  + jnp.dot(rec_ref[...], wr_ref[...], preferred_element_type=F32))


def _out_proj(x, att, rec, w_out, layer):
    n, d = x.shape
    wa = att.shape[1]
    wr = rec.shape[1]
    tm = min(OUT_TOKENS, n)
    assert n % tm == 0
    row = lambda i: (i, 0)
    return pl.pallas_call(
        _out_kernel,
        out_shape=jax.ShapeDtypeStruct((n, d), F32),
        grid=(n // tm,),
        in_specs=[pl.BlockSpec((tm, d), row), pl.BlockSpec((tm, wa), row),
                  pl.BlockSpec((tm, wr), row),
                  pl.BlockSpec((None, None, wa, d), lambda i: (layer, 0, 0, 0)),
                  pl.BlockSpec((None, None, wr, d), lambda i: (layer, 0, 1, 0))],
        out_specs=pl.BlockSpec((tm, d), row),
        compiler_params=_params(("parallel",)),
        name="out_proj",
    )(x, att, rec, w_out, w_out)


def kernel(x, ffn1_norm, ffn1_w_gate, ffn1_w_up, ffn1_w_down, mix_norm, w_in, w_out, lambda_q1, lambda_k1, lambda_q2, lambda_k2, diff_subln, hgrn_lower_bounds, hgrn_gnorm, ffn2_norm, ffn2_w_gate, ffn2_w_up, ffn2_w_down, final_norm):
    batch, seq, d = x.shape
    depth = w_in.shape[0]
    assert w_out.shape[1] == 2 * w_in.shape[2] // N_IN_PARTS

    tables = _rope_tables(seq)
    lb_raw = hgrn_lower_bounds.astype(F32)
    tf = min(FFN_HIDDEN_BLOCK, ffn1_w_gate.shape[2])
    f1 = (_to_bf16_blocks(ffn1_w_gate, tf), _to_bf16_blocks(ffn1_w_up, tf),
          _to_bf16_blocks(ffn1_w_down))
    f2 = (_to_bf16_blocks(ffn2_w_gate, tf), _to_bf16_blocks(ffn2_w_up, tf),
          _to_bf16_blocks(ffn2_w_down))
    w_in_b = _to_bf16_blocks(w_in, w_in.shape[2] // N_IN_PARTS)
    w_out_b = _to_bf16_blocks(w_out)

    h = x.reshape(batch * seq, d)
    for l in range(depth):
        lambda_init = 0.8 - 0.6 * math.exp(-0.3 * l)
        h = _ffn(h, ffn1_norm[l], *f1, l)
        q, k, vt, rq, lf, kk, ri, rg = _proj(h, mix_norm[l], w_in_b, tables, lb_raw, l, seq)
        att = _attention(q, k, vt, lambda_q1[l], lambda_k1[l], lambda_q2[l], lambda_k2[l],
                         diff_subln[l], batch, seq, lambda_init)
        rec = _hgrn2(rq, lf, kk, ri, rg, hgrn_gnorm[l], batch, seq)
        h = _out_proj(h, att, rec, w_out_b, l)
        h = _ffn(h, ffn2_norm[l], *f2, l, final_w=final_norm if l == depth - 1 else None)
    return h.reshape(batch, seq, d)
```

```python
import functools
import math

import jax
import jax.numpy as jnp
import numpy as np
from jax import lax
from jax.experimental import pallas as pl
from jax.experimental.pallas import tpu as pltpu

NORM_EPS = 1e-6
ROPE_THETA = 10000.0
ATT_HEAD_DIM = 64
HEAD_WIDTH = 128
N_IN_PARTS = 7
LANES = 128
V7X_VMEM_LIMIT_BYTES = 60 * 1024 * 1024
CAST_BLOCK_BYTES = 6 * 1024 * 1024

FFN_TOKENS = 1024
FFN_HIDDEN_BLOCK = 512
PROJ_TOKENS = 256
OUT_TOKENS = 512
ATT_Q_BLOCK = 512
ATT_KV_BLOCK = 512
ATT_SUM_ROWS = 16
ATT_STAGES = 3
REC_BLOCK = 128
REC_STAGES = 3
REC_STEP_TOKENS = 512

F32 = jnp.float32
BF16 = jnp.bfloat16


def _rms(x, w):
    return x * lax.rsqrt(jnp.mean(x * x, axis=-1, keepdims=True) + NORM_EPS) * w


def _silu(x):
    return x * jax.nn.sigmoid(x)


def _params(semantics):
    return pltpu.CompilerParams(dimension_semantics=semantics,
                                vmem_limit_bytes=V7X_VMEM_LIMIT_BYTES)


def _cast_kernel(w_ref, o_ref, *, cols):
    block = w_ref.shape[1]
    w = w_ref[...]
    if cols % block:
        lane = lax.broadcasted_iota(jnp.int32, w.shape, 1)
        w = jnp.where(pl.program_id(1) * block + lane < cols, w, 0.0)
    o_ref[...] = w.astype(o_ref.dtype)


def _to_bf16_blocks(w, col_block=None):
    depth, rows, cols = w.shape
    col_block = cols if col_block is None else col_block
    n_blocks = pl.cdiv(cols, col_block)
    fits = [r for r in range(16, rows + 1, 16)
            if rows % r == 0 and r * col_block * w.dtype.itemsize <= CAST_BLOCK_BYTES]
    rb = max(fits)
    return pl.pallas_call(
        functools.partial(_cast_kernel, cols=cols),
        out_shape=jax.ShapeDtypeStruct((depth, n_blocks, rows, col_block), BF16),
        grid=(depth, n_blocks, rows // rb),
        in_specs=[pl.BlockSpec((None, rb, col_block), lambda l, c, r: (l, r, c))],
        out_specs=pl.BlockSpec((None, None, rb, col_block), lambda l, c, r: (l, c, r, 0)),
        compiler_params=_params(("parallel", "parallel", "parallel")),
        name="to_bf16",
    )(w)


def _ffn_kernel(x_ref, nw_ref, wg_ref, wu_ref, wd_ref, *rest, final, tail):
    if final:
        fw_ref, o_ref, hn_ref = rest
    else:
        o_ref, hn_ref = rest
    j = pl.program_id(1)
    last = pl.num_programs(1) - 1
    tf = wg_ref.shape[1]

    @pl.when(j == 0)
    def _():
        x = x_ref[...]
        hn_ref[...] = _rms(x, nw_ref[...]).astype(BF16)
        o_ref[...] = x

    def hidden_block(width):
        h = hn_ref[...]
        g = jnp.dot(h, wg_ref[:, :width], preferred_element_type=F32)
        u = jnp.dot(h, wu_ref[:, :width], preferred_element_type=F32)
        a = (0.5 * _silu(g) * u).astype(BF16)
        o_ref[...] += jnp.dot(a, wd_ref[:width, :], preferred_element_type=F32)

    if tail == tf:
        hidden_block(tf)
    else:
        pl.when(j == 0)(functools.partial(hidden_block, tail))
        pl.when(j > 0)(functools.partial(hidden_block, tf))

    if final:
        @pl.when(j == last)
        def _():
            o_ref[...] = _rms(o_ref[...], fw_ref[...])


def _ffn(x, norm_w, wg, wu, wd, layer, final_w=None):
    n, d = x.shape
    f = wd.shape[2]
    n_hidden, tf = wg.shape[1], wg.shape[3]
    tm = min(FFN_TOKENS, n)
    assert n % tm == 0 and n_hidden == pl.cdiv(f, tf)
    tail = f - (n_hidden - 1) * tf
    final = final_w is not None
    row = lambda i, j: (i, 0)
    vec = pl.BlockSpec((1, d), lambda i, j: (0, 0))
    hidden = (lambda j: j) if tail == tf else (lambda j: (j + n_hidden - 1) % n_hidden)
    in_specs = [pl.BlockSpec((tm, d), row), vec,
                pl.BlockSpec((None, None, d, tf), lambda i, j: (layer, hidden(j), 0, 0)),
                pl.BlockSpec((None, None, d, tf), lambda i, j: (layer, hidden(j), 0, 0)),
                pl.BlockSpec((None, None, tf, d), lambda i, j: (layer, 0, hidden(j), 0))]
    args = [x, norm_w.reshape(1, d), wg, wu, wd]
    if final:
        in_specs.append(vec)
        args.append(final_w.reshape(1, d))
    return pl.pallas_call(
        functools.partial(_ffn_kernel, final=final, tail=tail),
        out_shape=jax.ShapeDtypeStruct((n, d), F32),
        grid=(n // tm, n_hidden),
        in_specs=in_specs,
        out_specs=pl.BlockSpec((tm, d), row),
        scratch_shapes=[pltpu.VMEM((tm, d), BF16)],
        compiler_params=_params(("parallel", "arbitrary")),
        name="ffn_final" if final else "ffn",
    )(*args)


def _rope(y, cos, sin_lo, sin_hi):
    width = y.shape[1]
    from_hi = pltpu.roll(y, width - ATT_HEAD_DIM // 2, 1)
    from_lo = pltpu.roll(y, ATT_HEAD_DIM // 2, 1)
    reps = width // LANES
    tile = lambda t: jnp.concatenate([t] * reps, axis=1)
    return y * tile(cos) + from_hi * tile(sin_lo) + from_lo * tile(sin_hi)


def _proj_kernel(x_ref, nw_ref, w_ref, cos_ref, slo_ref, shi_ref, lbraw_ref,
                 q_o, k_o, vt_o, rq_o, lf_o, kk_o, ri_o, rg_o, hn_ref, y_scr, *, layer):
    hn_ref[...] = _rms(x_ref[...], nw_ref[...]).astype(BF16)

    def finish(part, y):
        if part == 0:
            scale = ATT_HEAD_DIM ** -0.5 * math.log2(math.e)
            q_o[...] = (_rope(y, cos_ref[...], slo_ref[...], shi_ref[...]) * scale).astype(BF16)
        elif part == 1:
            k_o[...] = _rope(y, cos_ref[...], slo_ref[...], shi_ref[...]).astype(BF16)
        elif part == 2:
            vt_o[...] = y.T.astype(BF16)
        elif part == 3:
            rq_o[...] = _silu(y)
        elif part == 4:
            raw = lbraw_ref[...]
            e = jnp.exp(raw - jnp.max(raw, axis=0, keepdims=True))
            sm = e / jnp.sum(e, axis=0, keepdims=True)
            lb = jnp.sum(sm[:layer + 1], axis=0, keepdims=True) - sm[0:1]
            lf_o[...] = jnp.log(lb + (1.0 - lb) * jax.nn.sigmoid(y))
            kk_o[...] = (1.0 - lb) * jax.nn.sigmoid(-y)
        elif part == 5:
            ri_o[...] = y.astype(BF16)
        else:
            rg_o[...] = _silu(y)

    for s in range(N_IN_PARTS + 1):
        if s >= 1:
            finish(s - 1, y_scr[(s - 1) % 2])
        if s < N_IN_PARTS:
            y_scr[s % 2] = jnp.dot(hn_ref[...], w_ref[s], preferred_element_type=F32)


def _proj(x, norm_w, w_in, tables, lb_raw, layer, seq):
    n, d = x.shape
    width = w_in.shape[3]
    assert w_in.shape[1] == N_IN_PARTS
    tm = min(PROJ_TOKENS, seq)
    assert n % tm == 0 and seq % tm == 0
    pos_blocks = seq // tm
    depth = lb_raw.shape[0]
    row = lambda i: (i, 0)
    tab = pl.BlockSpec((tm, LANES), lambda i: (i % pos_blocks, 0))
    out_dtypes = (BF16, BF16, BF16, F32, F32, F32, BF16, F32)
    return pl.pallas_call(
        functools.partial(_proj_kernel, layer=layer),
        out_shape=[jax.ShapeDtypeStruct((width, n) if o == 2 else (n, width), dt)
                   for o, dt in enumerate(out_dtypes)],
        grid=(n // tm,),
        in_specs=[pl.BlockSpec((tm, d), row),
                  pl.BlockSpec((1, d), lambda i: (0, 0)),
                  pl.BlockSpec((None, N_IN_PARTS, d, width), lambda i: (layer, 0, 0, 0),
                               pipeline_mode=pl.Buffered(1)),
                  tab, tab, tab,
                  pl.BlockSpec((depth, width), lambda i: (0, 0))],
        out_specs=[pl.BlockSpec((width, tm), lambda i: (0, i)) if o == 2
                   else pl.BlockSpec((tm, width), row) for o in range(len(out_dtypes))],
        scratch_shapes=[pltpu.VMEM((tm, d), BF16), pltpu.VMEM((2, tm, width), F32)],
        compiler_params=_params(("parallel",)),
        name="in_proj",
    )(x, norm_w.reshape(1, d), w_in, *tables, lb_raw)


def _rope_tables(seq):
    half = ATT_HEAD_DIM // 2
    inv_freq = ROPE_THETA ** (-jnp.arange(0, ATT_HEAD_DIM, 2, dtype=F32) / ATT_HEAD_DIM)
    ang = jnp.arange(seq, dtype=F32)[:, None] * inv_freq[None, :]
    cos, sin, zero = jnp.cos(ang), jnp.sin(ang), jnp.zeros_like(ang)
    reps = LANES // ATT_HEAD_DIM
    cos_t = jnp.tile(jnp.concatenate([cos, cos], axis=1), (1, reps))
    sin_lo = jnp.tile(jnp.concatenate([-sin, zero], axis=1), (1, reps))
    sin_hi = jnp.tile(jnp.concatenate([zero, sin], axis=1), (1, reps))
    del half
    return cos_t, sin_lo, sin_hi


def _att_schedule(seq, tq, tk):
    qi, kj, first, last, diag = [], [], [], [], []
    for i in range(seq // tq):
        jmax = ((i + 1) * tq - 1) // tk
        for j in range(jmax + 1):
            qi.append(i)
            kj.append(j)
            first.append(int(j == 0))
            last.append(int(j == jmax))
            diag.append(int((j + 1) * tk - 1 > i * tq))
    return [np.asarray(a, np.int32) for a in (qi, kj, first, last, diag)]


def _att_kernel(qi_ref, kj_ref, first_ref, last_ref, diag_ref,
                q_ref, k_ref, vt_ref, lq1_ref, lk1_ref, lq2_ref, lk2_ref, sw_ref,
                o_ref, qz_ref, m_ref, acc_ref, s_scr, p_scr, *, heads, tq, tk, lambda_init):
    t = pl.program_id(1)
    qi, kj = qi_ref[t], kj_ref[t]

    @pl.when(first_ref[t] == 1)
    def _():
        m_ref[...] = jnp.full(m_ref.shape, -jnp.inf, F32)
        acc_ref[...] = jnp.zeros(acc_ref.shape, F32)
        q = q_ref[...]
        low = lax.broadcasted_iota(jnp.int32, q.shape, 1) % HEAD_WIDTH < ATT_HEAD_DIM
        zero = jnp.zeros_like(q)
        qz_ref[0:tq, :] = jnp.where(low, q, zero)
        qz_ref[tq:2 * tq, :] = jnp.where(low, zero, q)

    ones_rows = jnp.ones((ATT_SUM_ROWS, tk), BF16)

    def step(masked):
        if masked:
            row = lax.broadcasted_iota(jnp.int32, (tk, 2 * tq), 0)
            col = lax.broadcasted_iota(jnp.int32, (tk, 2 * tq), 1)
            qpos = qi * tq + jnp.where(col >= tq, col - tq, col)
            keep = kj * tk + row <= qpos

        def scores(h):
            cols = slice(h * HEAD_WIDTH, (h + 1) * HEAD_WIDTH)
            s = lax.dot_general(k_ref[:, cols], qz_ref[:, cols], (((1,), (1,)), ((), ())),
                                preferred_element_type=F32)
            if masked:
                s = jnp.where(keep, s, -jnp.inf)
            s_scr[h % ATT_STAGES] = s
            return jnp.max(s, axis=0, keepdims=True)

        def probs(h, m_cur):
            m_prev = m_ref[h]
            m_new = jnp.maximum(m_prev, m_cur)
            p_scr[h % ATT_STAGES] = jnp.exp2(s_scr[h % ATT_STAGES] - m_new).astype(BF16)
            m_ref[h] = m_new
            return jnp.exp2(m_prev - m_new)

        def values(h, alpha):
            cols = slice(h * HEAD_WIDTH, (h + 1) * HEAD_WIDTH)
            vt_ext = jnp.concatenate([vt_ref[cols, :], ones_rows], axis=0)
            acc_ref[h] = alpha * acc_ref[h] + jnp.dot(vt_ext, p_scr[h % ATT_STAGES],
                                                      preferred_element_type=F32)

        ahead = ATT_STAGES - 1
        m_cur = {h: scores(h) for h in range(min(ahead, heads))}
        for h in range(heads):
            if h + ahead < heads:
                m_cur[h + ahead] = scores(h + ahead)
            values(h, probs(h, m_cur.pop(h)))

    @pl.when(diag_ref[t] == 1)
    def _():
        step(True)

    @pl.when(diag_ref[t] == 0)
    def _():
        step(False)

    @pl.when(last_ref[t] == 1)
    def _():
        lam = (jnp.exp(jnp.sum(lq1_ref[...] * lk1_ref[...], axis=1, keepdims=True))
               - jnp.exp(jnp.sum(lq2_ref[...] * lk2_ref[...], axis=1, keepdims=True))
               + lambda_init)
        for h in range(heads):
            a = acc_ref[h]
            o12 = a[0:HEAD_WIDTH] / a[HEAD_WIDTH:HEAD_WIDTH + 1]
            o = (o12[:, 0:tq] - lam * o12[:, tq:2 * tq]).T
            o = _rms(o, sw_ref[...]) * (1.0 - lambda_init)
            o_ref[:, h * HEAD_WIDTH:(h + 1) * HEAD_WIDTH] = o.astype(o_ref.dtype)


def _attention(q, k, vt, lq1, lk1, lq2, lk2, subln_w, batch, seq, lambda_init):
    n, width = q.shape
    heads = width // HEAD_WIDTH
    tq = min(ATT_Q_BLOCK, seq)
    tk = min(ATT_KV_BLOCK, seq)
    assert seq % tq == 0 and seq % tk == 0
    sched = _att_schedule(seq, tq, tk)
    nq, nk = seq // tq, seq // tk
    vec = lambda m: pl.BlockSpec((1, m), lambda b, t, *_: (0, 0))
    grid_spec = pltpu.PrefetchScalarGridSpec(
        num_scalar_prefetch=len(sched),
        grid=(batch, len(sched[0])),
        in_specs=[pl.BlockSpec((tq, width), lambda b, t, qi, kj, *_: (b * nq + qi[t], 0)),
                  pl.BlockSpec((tk, width), lambda b, t, qi, kj, *_: (b * nk + kj[t], 0)),
                  pl.BlockSpec((width, tk), lambda b, t, qi, kj, *_: (0, b * nk + kj[t])),
                  vec(ATT_HEAD_DIM), vec(ATT_HEAD_DIM), vec(ATT_HEAD_DIM), vec(ATT_HEAD_DIM),
                  vec(HEAD_WIDTH)],
        out_specs=pl.BlockSpec((tq, width), lambda b, t, qi, kj, *_: (b * nq + qi[t], 0)),
        scratch_shapes=[pltpu.VMEM((2 * tq, width), BF16),
                        pltpu.VMEM((heads, 1, 2 * tq), F32),
                        pltpu.VMEM((heads, HEAD_WIDTH + ATT_SUM_ROWS, 2 * tq), F32),
                        pltpu.VMEM((ATT_STAGES, tk, 2 * tq), F32),
                        pltpu.VMEM((ATT_STAGES, tk, 2 * tq), BF16)],
    )
    r = lambda a: a.reshape(1, -1)
    return pl.pallas_call(
        functools.partial(_att_kernel, heads=heads, tq=tq, tk=tk, lambda_init=lambda_init),
        out_shape=jax.ShapeDtypeStruct((n, width), BF16),
        grid_spec=grid_spec,
        compiler_params=_params(("parallel", "arbitrary")),
        name="diff_attention",
    )(*sched, q, k, vt, r(lq1), r(lk1), r(lq2), r(lk2), r(subln_w))


def _rec_levels(t):
    return [t >> s for s in range(1, t.bit_length())]


def _rec_kernel(tri_ref, q_ref, lf_ref, kk_ref, v_ref, g_ref, gw_ref, o_ref, state_ref,
                b_scr, qd_scr, kd_scr, sc_scr, *, heads, t):
    @pl.when(pl.program_id(1) == 0)
    def _():
        state_ref[...] = jnp.zeros(state_ref.shape, F32)

    levels = _rec_levels(t)
    n_lv = len(levels)
    tok = lax.broadcasted_iota(jnp.int32, (t, HEAD_WIDTH), 0)
    row = lax.broadcasted_iota(jnp.int32, (t, t), 0)
    col = lax.broadcasted_iota(jnp.int32, (t, t), 1)
    split = row ^ col
    trans_b = (((1,), (1,)), ((), ()))
    head_cols = lambda h: slice(h * HEAD_WIDTH, (h + 1) * HEAD_WIDTH)

    def token_block(c, carry):
        rows = pl.ds(pl.multiple_of(c * t, t), t)

        def cumulative_decay(h):
            lf = lf_ref[rows, head_cols(h)]
            lf_hi = lf.astype(BF16)
            lf_lo = (lf - lf_hi.astype(F32)).astype(BF16)
            b2 = jnp.dot(tri_ref[...], jnp.concatenate([lf_hi, lf_lo], axis=1),
                         preferred_element_type=F32)
            b_scr[h] = b2[:, :HEAD_WIDTH] + b2[:, HEAD_WIDTH:]

        def level_exponent(h, w, lf):
            late = (tok & w) != 0
            if w == 1:
                return jnp.where(late, lf, 0.0)
            if w == 2:
                nxt = pltpu.roll(lf, t - 1, 0)
                prv = pltpu.roll(lf, 1, 0)
                odd = (tok & 1) != 0
                return jnp.where(late, jnp.where(odd, lf + prv, lf), jnp.where(odd, 0.0, nxt))
            d = jnp.concatenate([b_scr[h, r0:r0 + 2 * w] - b_scr[h, r0 + w - 1:r0 + w]
                                 for r0 in range(0, t, 2 * w)], axis=0)
            return jnp.where(late, d, -d)

        def operands(h):
            q, kk = q_ref[rows, head_cols(h)], kk_ref[rows, head_cols(h)]
            lf = lf_ref[rows, head_cols(h)]
            s = h % REC_STAGES
            qd_scr[s, 0] = q.astype(BF16)
            kd_scr[s, 0] = kk.astype(BF16)
            for idx, w in enumerate(levels):
                late = (tok & w) != 0
                mixed = jnp.where(late, q, kk) * jnp.exp(level_exponent(h, w, lf))
                qd_scr[s, idx + 1] = mixed.astype(BF16)
            b = b_scr[h]
            qd_scr[s, n_lv + 1] = (q * jnp.exp(b)).astype(BF16)
            kd_scr[s, 1] = (kk * jnp.exp(b_scr[h, t - 1:t] - b)).astype(BF16)

        def block_scores(h):
            s = h % REC_STAGES
            scores = lax.dot_general(qd_scr[s, 0], kd_scr[s, 0], trans_b,
                                     preferred_element_type=F32)
            for idx, w in reversed(list(enumerate(levels))):
                mixed = qd_scr[s, idx + 1]
                p = lax.dot_general(mixed, mixed, trans_b, preferred_element_type=F32)
                scores = jnp.where(split >= w, p, scores)
            sc_scr[h % 2] = jnp.where(row >= col, scores, 0.0).astype(BF16)

        def outputs(h):
            s = h % REC_STAGES
            v = v_ref[rows, head_cols(h)]
            st = state_ref[h]
            o = (lax.dot_general(qd_scr[s, n_lv + 1], st.astype(BF16), trans_b,
                                 preferred_element_type=F32)
                 + jnp.dot(sc_scr[h % 2], v, preferred_element_type=F32))
            upd = lax.dot_general(v, kd_scr[s, 1], (((0,), (0,)), ((), ())),
                                  preferred_element_type=F32)
            state_ref[h] = st * jnp.exp(b_scr[h, t - 1:t]) + upd
            o = _rms(o, gw_ref[...]) * g_ref[rows, head_cols(h)]
            o_ref[rows, head_cols(h)] = o.astype(o_ref.dtype)

        for h in range(heads):
            cumulative_decay(h)
        for h in range(min(2, heads)):
            operands(h)
        block_scores(0)
        for h in range(heads):
            if h + 2 < heads:
                operands(h + 2)
            if h + 1 < heads:
                block_scores(h + 1)
            outputs(h)
        return carry

    lax.fori_loop(0, q_ref.shape[0] // t, token_block, 0)


def _hgrn2(rq, lf, kk, ri, rg, gnorm_w, batch, seq):
    n, width = rq.shape
    heads = width // HEAD_WIDTH
    t = min(REC_BLOCK, seq)
    assert seq % t == 0 and t & (t - 1) == 0 and t >= 16
    ts = min(REC_STEP_TOKENS, seq)
    assert seq % ts == 0 and ts % t == 0
    nb = seq // ts
    tri =jnp.asarray(np.tril(np.ones((t, t), np.float32)), BF16)
    n_lv = len(_rec_levels(t))
    blk = pl.BlockSpec((ts, width), lambda b, c: (b * nb + c, 0))
    return pl.pallas_call(
        functools.partial(_rec_kernel, heads=heads, t=t),
        out_shape=jax.ShapeDtypeStruct((n, width), BF16),
        grid=(batch, nb),
        in_specs=[pl.BlockSpec((t, t), lambda b, c: (0, 0)), blk, blk, blk, blk, blk,
                  pl.BlockSpec((1, HEAD_WIDTH), lambda b, c: (0, 0))],
        out_specs=blk,
        scratch_shapes=[pltpu.VMEM((heads, HEAD_WIDTH, HEAD_WIDTH), F32),
                        pltpu.VMEM((heads, t, HEAD_WIDTH), F32),
                        pltpu.VMEM((REC_STAGES, n_lv + 2, t, HEAD_WIDTH), BF16),
                        pltpu.VMEM((REC_STAGES, 2, t, HEAD_WIDTH), BF16),
                        pltpu.VMEM((2, t, t), BF16)],
        compiler_params=_params(("parallel", "arbitrary")),
        name="hgrn2",
    )(tri, rq, lf, kk, ri, rg, gnorm_w.reshape(1, -1))


def _out_kernel(x_ref, att_ref, rec_ref, wa_ref, wr_ref, o_ref):
    o_ref[...] = (x_ref[...]
                  + jnp.dot(att_ref[...], wa_ref[...], preferred_element_type=F32)
                  + jnp.dot(rec_ref[...], wr_ref[...], preferred_element_type=F32))


def _out_proj(x, att, rec, w_out, layer):
    n, d = x.shape
    wa = att.shape[1]
    wr = rec.shape[1]
    tm = min(OUT_TOKENS, n)
    assert n % tm == 0
    row = lambda i: (i, 0)
    return pl.pallas_call(
        _out_kernel,
        out_shape=jax.ShapeDtypeStruct((n, d), F32),
        grid=(n // tm,),
        in_specs=[pl.BlockSpec((tm, d), row), pl.BlockSpec((tm, wa), row),
                  pl.BlockSpec((tm, wr), row),
                  pl.BlockSpec((None, None, wa, d), lambda i: (layer, 0, 0, 0)),
                  pl.BlockSpec((None, None, wr, d), lambda i: (layer, 0, 1, 0))],
        out_specs=pl.BlockSpec((tm, d), row),
        compiler_params=_params(("parallel",)),
        name="out_proj",
    )(x, att, rec, w_out, w_out)


def kernel(x, ffn1_norm, ffn1_w_gate, ffn1_w_up, ffn1_w_down, mix_norm, w_in, w_out, lambda_q1, lambda_k1, lambda_q2, lambda_k2, diff_subln, hgrn_lower_bounds, hgrn_gnorm, ffn2_norm, ffn2_w_gate, ffn2_w_up, ffn2_w_down, final_norm):
    batch, seq, d = x.shape
    depth = w_in.shape[0]
    assert w_out.shape[1] == 2 * w_in.shape[2] // N_IN_PARTS

    tables = _rope_tables(seq)
    lb_raw = hgrn_lower_bounds.astype(F32)
    tf = min(FFN_HIDDEN_BLOCK, ffn1_w_gate.shape[2])
    f1 = (_to_bf16_blocks(ffn1_w_gate, tf), _to_bf16_blocks(ffn1_w_up, tf),
          _to_bf16_blocks(ffn1_w_down))
    f2 = (_to_bf16_blocks(ffn2_w_gate, tf), _to_bf16_blocks(ffn2_w_up, tf),
          _to_bf16_blocks(ffn2_w_down))
    w_in_b = _to_bf16_blocks(w_in, w_in.shape[2] // N_IN_PARTS)
    w_out_b = _to_bf16_blocks(w_out)

    h = x.reshape(batch * seq, d)
    for l in range(depth):
        lambda_init = 0.8 - 0.6 * math.exp(-0.3 * l)
        h = _ffn(h, ffn1_norm[l], *f1, l)
        q, k, vt, rq, lf, kk, ri, rg = _proj(h, mix_norm[l], w_in_b, tables, lb_raw, l, seq)
        att = _attention(q, k, vt, lambda_q1[l], lambda_k1[l], lambda_q2[l], lambda_k2[l],
                         diff_subln[l], batch, seq, lambda_init)
        rec = _hgrn2(rq, lf, kk, ri, rg, hgrn_gnorm[l], batch, seq)
        h = _out_proj(h, att, rec, w_out_b, l)
        h = _ffn(h, ffn2_norm[l], *f2, l, final_w=final_norm if l == depth - 1 else None)
    return h.reshape(batch, seq, d)
```

```python
import functools
import math

import jax
import jax.numpy as jnp
import numpy as np
from jax import lax
from jax.experimental import pallas as pl
from jax.experimental.pallas import tpu as pltpu

NORM_EPS = 1e-6
ROPE_THETA = 10000.0
ATT_HEAD_DIM = 64
HEAD_WIDTH = 128
N_IN_PARTS = 7
LANES = 128
V7X_VMEM_LIMIT_BYTES = 60 * 1024 * 1024
CAST_BLOCK_BYTES = 6 * 1024 * 1024

FFN_TOKENS = 1024
FFN_HIDDEN_BLOCK = 512
PROJ_TOKENS = 256
OUT_TOKENS = 512
ATT_Q_BLOCK = 512
ATT_KV_BLOCK = 512
ATT_SUM_ROWS = 16
ATT_STAGES = 3
REC_BLOCK = 128
REC_STAGES = 3
REC_STEP_TOKENS = 512

F32 = jnp.float32
BF16 = jnp.bfloat16


def _rms(x, w):
    return x * lax.rsqrt(jnp.mean(x * x, axis=-1, keepdims=True) + NORM_EPS) * w


def _silu(x):
    return x * jax.nn.sigmoid(x)


def _params(semantics):
    return pltpu.CompilerParams(dimension_semantics=semantics,
                                vmem_limit_bytes=V7X_VMEM_LIMIT_BYTES)


def _cast_kernel(w_ref, o_ref, *, cols):
    block = w_ref.shape[1]
    w = w_ref[...]
    if cols % block:
        lane = lax.broadcasted_iota(jnp.int32, w.shape, 1)
        w = jnp.where(pl.program_id(1) * block + lane < cols, w, 0.0)
    o_ref[...] = w.astype(o_ref.dtype)


def _to_bf16_blocks(w, col_block=None):
    depth, rows, cols = w.shape
    col_block = cols if col_block is None else col_block
    n_blocks = pl.cdiv(cols, col_block)
    fits = [r for r in range(16, rows + 1, 16)
            if rows % r == 0 and r * col_block * w.dtype.itemsize <= CAST_BLOCK_BYTES]
    rb = max(fits)
    return pl.pallas_call(
        functools.partial(_cast_kernel, cols=cols),
        out_shape=jax.ShapeDtypeStruct((depth, n_blocks, rows, col_block), BF16),
        grid=(depth, n_blocks, rows // rb),
        in_specs=[pl.BlockSpec((None, rb, col_block), lambda l, c, r: (l, r, c))],
        out_specs=pl.BlockSpec((None, None, rb, col_block), lambda l, c, r: (l, c, r, 0)),
        compiler_params=_params(("parallel", "parallel", "parallel")),
        name="to_bf16",
    )(w)


def _ffn_kernel(x_ref, nw_ref, wg_ref, wu_ref, wd_ref, *rest, final, tail):
    if final:
        fw_ref, o_ref, hn_ref = rest
    else:
        o_ref, hn_ref = rest
    j = pl.program_id(1)
    last = pl.num_programs(1) - 1
    tf = wg_ref.shape[1]

    @pl.when(j == 0)
    def _():
        x = x_ref[...]
        hn_ref[...] = _rms(x, nw_ref[...]).astype(BF16)
        o_ref[...] = x

    def hidden_block(width):
        h = hn_ref[...]
        g = jnp.dot(h, wg_ref[:, :width], preferred_element_type=F32)
        u = jnp.dot(h, wu_ref[:, :width], preferred_element_type=F32)
        a = (0.5 * _silu(g) * u).astype(BF16)
        o_ref[...] += jnp.dot(a, wd_ref[:width, :], preferred_element_type=F32)

    if tail == tf:
        hidden_block(tf)
    else:
        pl.when(j == 0)(functools.partial(hidden_block, tail))
        pl.when(j > 0)(functools.partial(hidden_block, tf))

    if final:
        @pl.when(j == last)
        def _():
            o_ref[...] = _rms(o_ref[...], fw_ref[...])


def _ffn(x, norm_w, wg, wu, wd, layer, final_w=None):
    n, d = x.shape
    f = wd.shape[2]
    n_hidden, tf = wg.shape[1], wg.shape[3]
    tm = min(FFN_TOKENS, n)
    assert n % tm == 0 and n_hidden == pl.cdiv(f, tf)
    tail = f - (n_hidden - 1) * tf
    final = final_w is not None
    row = lambda i, j: (i, 0)
    vec = pl.BlockSpec((1, d), lambda i, j: (0, 0))
    hidden = (lambda j: j) if tail == tf else (lambda j: (j + n_hidden - 1) % n_hidden)
    in_specs = [pl.BlockSpec((tm, d), row), vec,
                pl.BlockSpec((None, None, d, tf), lambda i, j: (layer, hidden(j), 0, 0)),
                pl.BlockSpec((None, None, d, tf), lambda i, j: (layer, hidden(j), 0, 0)),
                pl.BlockSpec((None, None, tf, d), lambda i, j: (layer, 0, hidden(j), 0))]
    args = [x, norm_w.reshape(1, d), wg, wu, wd]
    if final:
        in_specs.append(vec)
        args.append(final_w.reshape(1, d))
    return pl.pallas_call(
        functools.partial(_ffn_kernel, final=final, tail=tail),
        out_shape=jax.ShapeDtypeStruct((n, d), F32),
        grid=(n // tm, n_hidden),
        in_specs=in_specs,
        out_specs=pl.BlockSpec((tm, d), row),
        scratch_shapes=[pltpu.VMEM((tm, d), BF16)],
        compiler_params=_params(("parallel", "arbitrary")),
        name="ffn_final" if final else "ffn",
    )(*args)


def _rope(y, cos, sin_lo, sin_hi):
    width = y.shape[1]
    from_hi = pltpu.roll(y, width - ATT_HEAD_DIM // 2, 1)
    from_lo = pltpu.roll(y, ATT_HEAD_DIM // 2, 1)
    reps = width // LANES
    tile = lambda t: jnp.concatenate([t] * reps, axis=1)
    return y * tile(cos) + from_hi * tile(sin_lo) + from_lo * tile(sin_hi)


def _proj_kernel(x_ref, nw_ref, w_ref, cos_ref, slo_ref, shi_ref, lbraw_ref,
                 q_o, k_o, vt_o, rq_o, lf_o, kk_o, ri_o, rg_o, hn_ref, y_scr, *, layer):
    hn_ref[...] = _rms(x_ref[...], nw_ref[...]).astype(BF16)

    def finish(part, y):
        if part == 0:
            scale = ATT_HEAD_DIM ** -0.5 * math.log2(math.e)
            q_o[...] = (_rope(y, cos_ref[...], slo_ref[...], shi_ref[...]) * scale).astype(BF16)
        elif part == 1:
            k_o[...] = _rope(y, cos_ref[...], slo_ref[...], shi_ref[...]).astype(BF16)
        elif part == 2:
            vt_o[...] = y.T.astype(BF16)
        elif part == 3:
            rq_o[...] = _silu(y)
        elif part == 4:
            raw = lbraw_ref[...]
            e = jnp.exp(raw - jnp.max(raw, axis=0, keepdims=True))
            sm = e / jnp.sum(e, axis=0, keepdims=True)
            lb = jnp.sum(sm[:layer + 1], axis=0, keepdims=True) - sm[0:1]
            lf_o[...] = jnp.log(lb + (1.0 - lb) * jax.nn.sigmoid(y))
            kk_o[...] = (1.0 - lb) * jax.nn.sigmoid(-y)
        elif part == 5:
            ri_o[...] = y.astype(BF16)
        else:
            rg_o[...] = _silu(y)

    for s in range(N_IN_PARTS + 1):
        if s >= 1:
            finish(s - 1, y_scr[(s - 1) % 2])
        if s < N_IN_PARTS:
            y_scr[s % 2] = jnp.dot(hn_ref[...], w_ref[s], preferred_element_type=F32)


def _proj(x, norm_w, w_in, tables, lb_raw, layer, seq):
    n, d = x.shape
    width = w_in.shape[3]
    assert w_in.shape[1] == N_IN_PARTS
    tm = min(PROJ_TOKENS, seq)
    assert n % tm == 0 and seq % tm == 0
    pos_blocks = seq // tm
    depth = lb_raw.shape[0]
    row = lambda i: (i, 0)
    tab = pl.BlockSpec((tm, LANES), lambda i: (i % pos_blocks, 0))
    out_dtypes = (BF16, BF16, BF16, F32, F32, F32, BF16, F32)
    return pl.pallas_call(
        functools.partial(_proj_kernel, layer=layer),
        out_shape=[jax.ShapeDtypeStruct((width, n) if o == 2 else (n, width), dt)
                   for o, dt in enumerate(out_dtypes)],
        grid=(n // tm,),
        in_specs=[pl.BlockSpec((tm, d), row),
                  pl.BlockSpec((1, d), lambda i: (0, 0)),
                  pl.BlockSpec((None, N_IN_PARTS, d, width), lambda i: (layer, 0, 0, 0),
                               pipeline_mode=pl.Buffered(1)),
                  tab, tab, tab,
                  pl.BlockSpec((depth, width), lambda i: (0, 0))],
        out_specs=[pl.BlockSpec((width, tm), lambda i: (0, i)) if o == 2
                   else pl.BlockSpec((tm, width), row) for o in range(len(out_dtypes))],
        scratch_shapes=[pltpu.VMEM((tm, d), BF16), pltpu.VMEM((2, tm, width), F32)],
        compiler_params=_params(("parallel",)),
        name="in_proj",
    )(x, norm_w.reshape(1, d), w_in, *tables, lb_raw)


def _rope_tables(seq):
    half = ATT_HEAD_DIM // 2
    inv_freq = ROPE_THETA ** (-jnp.arange(0, ATT_HEAD_DIM, 2, dtype=F32) / ATT_HEAD_DIM)
    ang = jnp.arange(seq, dtype=F32)[:, None] * inv_freq[None, :]
    cos, sin, zero = jnp.cos(ang), jnp.sin(ang), jnp.zeros_like(ang)
    reps = LANES // ATT_HEAD_DIM
    cos_t = jnp.tile(jnp.concatenate([cos, cos], axis=1), (1, reps))
    sin_lo = jnp.tile(jnp.concatenate([-sin, zero], axis=1), (1, reps))
    sin_hi = jnp.tile(jnp.concatenate([zero, sin], axis=1), (1, reps))
    del half
    return cos_t, sin_lo, sin_hi


def _att_schedule(seq, tq, tk):
    qi, kj, first, last, diag = [], [], [], [], []
    for i in range(seq // tq):
        jmax = ((i + 1) * tq - 1) // tk
        for j in range(jmax + 1):
            qi.append(i)
            kj.append(j)
            first.append(int(j == 0))
            last.append(int(j == jmax))
            diag.append(int((j + 1) * tk - 1 > i * tq))
    return [np.asarray(a, np.int32) for a in (qi, kj, first, last, diag)]


def _att_kernel(qi_ref, kj_ref, first_ref, last_ref, diag_ref,
                q_ref, k_ref, vt_ref, par_ref,
                o_ref, qz_ref, m_ref, acc_ref, s_scr, p_scr, *, heads, tq, tk, lambda_init):
    t = pl.program_id(1)
    qi, kj = qi_ref[t], kj_ref[t]

    @pl.when(first_ref[t] == 1)
    def _():
        m_ref[...] = jnp.full(m_ref.shape, -jnp.inf, F32)
        acc_ref[...] = jnp.zeros(acc_ref.shape, F32)
        q = q_ref[...]
        low = lax.broadcasted_iota(jnp.int32, q.shape, 1) % HEAD_WIDTH < ATT_HEAD_DIM
        zero = jnp.zeros_like(q)
        qz_ref[0:tq, :] = jnp.where(low, q, zero)
        qz_ref[tq:2 * tq, :] = jnp.where(low, zero, q)

    ones_rows = jnp.ones((ATT_SUM_ROWS, tk), BF16)

    def step(masked):
        if masked:
            row = lax.broadcasted_iota(jnp.int32, (tk, 2 * tq), 0)
            col = lax.broadcasted_iota(jnp.int32, (tk, 2 * tq), 1)
            qpos = qi * tq + jnp.where(col >= tq, col - tq, col)
            keep = kj * tk + row <= qpos

        def scores(h):
            cols = slice(h * HEAD_WIDTH, (h + 1) * HEAD_WIDTH)
            s = lax.dot_general(k_ref[:, cols], qz_ref[:, cols], (((1,), (1,)), ((), ())),
                                preferred_element_type=F32)
            if masked:
                s = jnp.where(keep, s, -jnp.inf)
            s_scr[h % ATT_STAGES] = s
            return jnp.max(s, axis=0, keepdims=True)

        def probs(h, m_cur):
            m_prev = m_ref[h]
            m_new = jnp.maximum(m_prev, m_cur)
            p_scr[h % ATT_STAGES] = jnp.exp2(s_scr[h % ATT_STAGES] - m_new).astype(BF16)
            m_ref[h] = m_new
            return jnp.exp2(m_prev - m_new)

        def values(h, alpha):
            cols = slice(h * HEAD_WIDTH, (h + 1) * HEAD_WIDTH)
            vt_ext = jnp.concatenate([vt_ref[cols, :], ones_rows], axis=0)
            acc_ref[h] = alpha * acc_ref[h] + jnp.dot(vt_ext, p_scr[h % ATT_STAGES],
                                                      preferred_element_type=F32)

        ahead = ATT_STAGES - 1
        m_cur = {h: scores(h) for h in range(min(ahead, heads))}
        for h in range(heads):
            if h + ahead < heads:
                m_cur[h + ahead] = scores(h + ahead)
            values(h, probs(h, m_cur.pop(h)))

    @pl.when(diag_ref[t] == 1)
    def _():
        step(True)

    @pl.when(diag_ref[t] == 0)
    def _():
        step(False)

    @pl.when(last_ref[t] == 1)
    def _():
        par = par_ref[...]
        lam = (jnp.exp(jnp.sum(par[0:1] * par[1:2], axis=1, keepdims=True))
               - jnp.exp(jnp.sum(par[2:3] * par[3:4], axis=1, keepdims=True))
               + lambda_init)
        for h in range(heads):
            a = acc_ref[h]
            o12 = a[0:HEAD_WIDTH] / a[HEAD_WIDTH:HEAD_WIDTH + 1]
            o = (o12[:, 0:tq] - lam * o12[:, tq:2 * tq]).T
            o = _rms(o, par[4:5]) * (1.0 - lambda_init)
            o_ref[:, h * HEAD_WIDTH:(h + 1) * HEAD_WIDTH] = o.astype(o_ref.dtype)


def _attention(q, k, vt, lq1, lk1, lq2, lk2, subln_w, batch, seq, lambda_init):
    n, width = q.shape
    heads = width // HEAD_WIDTH
    tq = min(ATT_Q_BLOCK, seq)
    tk = min(ATT_KV_BLOCK, seq)
    assert seq % tq == 0 and seq % tk == 0
    sched = _att_schedule(seq, tq, tk)
    nq, nk = seq // tq, seq // tk
    rows = [jnp.pad(v.astype(F32), (0, HEAD_WIDTH - v.shape[0])) for v in (lq1, lk1, lq2, lk2, subln_w)]
    par = jnp.stack(rows + [jnp.zeros((HEAD_WIDTH,), F32)] * (8 - len(rows)))
    grid_spec = pltpu.PrefetchScalarGridSpec(
        num_scalar_prefetch=len(sched),
        grid=(batch, len(sched[0])),
        in_specs=[pl.BlockSpec((tq, width), lambda b, t, qi, kj, *_: (b * nq + qi[t], 0)),
                  pl.BlockSpec((tk, width), lambda b, t, qi, kj, *_: (b * nk + kj[t], 0)),
                  pl.BlockSpec((width, tk), lambda b, t, qi, kj, *_: (0, b * nk + kj[t])),
                  pl.BlockSpec(par.shape, lambda b, t, *_: (0, 0))],
        out_specs=pl.BlockSpec((tq, width), lambda b, t, qi, kj, *_: (b * nq + qi[t], 0)),
        scratch_shapes=[pltpu.VMEM((2 * tq, width), BF16),
                        pltpu.VMEM((heads, 1, 2 * tq), F32),
                        pltpu.VMEM((heads, HEAD_WIDTH + ATT_SUM_ROWS, 2 * tq), F32),
                        pltpu.VMEM((ATT_STAGES, tk, 2 * tq), F32),
                        pltpu.VMEM((ATT_STAGES, tk, 2 * tq), BF16)],
    )
    return pl.pallas_call(
        functools.partial(_att_kernel, heads=heads, tq=tq, tk=tk, lambda_init=lambda_init),
        out_shape=jax.ShapeDtypeStruct((n, width), BF16),
        grid_spec=grid_spec,
        compiler_params=_params(("parallel", "arbitrary")),
        name="diff_attention",
    )(*sched, q, k, vt, par)


def _rec_levels(t):
    return [t >> s for s in range(1, t.bit_length())]


def _rec_kernel(tri_ref, q_ref, lf_ref, kk_ref, v_ref, g_ref, gw_ref, o_ref, state_ref,
                b_scr, qd_scr, kd_scr, sc_scr, *, heads, t):
    @pl.when(pl.program_id(1) == 0)
    def _():
        state_ref[...] = jnp.zeros(state_ref.shape, F32)

    levels = _rec_levels(t)
    n_lv = len(levels)
    tok = lax.broadcasted_iota(jnp.int32, (t, HEAD_WIDTH), 0)
    row = lax.broadcasted_iota(jnp.int32, (t, t), 0)
    col = lax.broadcasted_iota(jnp.int32, (t, t), 1)
    split = row ^ col
    trans_b = (((1,), (1,)), ((), ()))
    head_cols = lambda h: slice(h * HEAD_WIDTH, (h + 1) * HEAD_WIDTH)

    def token_block(c, carry):
        rows = pl.ds(pl.multiple_of(c * t, t), t)

        def cumulative_decay(h):
            lf = lf_ref[rows, head_cols(h)]
            lf_hi = lf.astype(BF16)
            lf_lo = (lf - lf_hi.astype(F32)).astype(BF16)
            b2 = jnp.dot(tri_ref[...], jnp.concatenate([lf_hi, lf_lo], axis=1),
                         preferred_element_type=F32)
            b_scr[h] = b2[:, :HEAD_WIDTH] + b2[:, HEAD_WIDTH:]

        def level_exponent(h, w, lf):
            late = (tok & w) != 0
            if w == 1:
                return jnp.where(late, lf, 0.0)
            if w == 2:
                nxt = pltpu.roll(lf, t - 1, 0)
                prv = pltpu.roll(lf, 1, 0)
                odd = (tok & 1) != 0
                return jnp.where(late, jnp.where(odd, lf + prv, lf), jnp.where(odd, 0.0, nxt))
            d = jnp.concatenate([b_scr[h, r0:r0 + 2 * w] - b_scr[h, r0 + w - 1:r0 + w]
                                 for r0 in range(0, t, 2 * w)], axis=0)
            return jnp.where(late, d, -d)

        def operands(h):
            q, kk = q_ref[rows, head_cols(h)], kk_ref[rows, head_cols(h)]
            lf = lf_ref[rows, head_cols(h)]
            s = h % REC_STAGES
            qd_scr[s, 0] = q.astype(BF16)
            kd_scr[s, 0] = kk.astype(BF16)
            for idx, w in enumerate(levels):
                late = (tok & w) != 0
                mixed = jnp.where(late, q, kk) * jnp.exp(level_exponent(h, w, lf))
                qd_scr[s, idx + 1] = mixed.astype(BF16)
            b = b_scr[h]
            qd_scr[s, n_lv + 1] = (q * jnp.exp(b)).astype(BF16)
            kd_scr[s, 1] = (kk * jnp.exp(b_scr[h, t - 1:t] - b)).astype(BF16)

        def block_scores(h):
            s = h % REC_STAGES
            scores = lax.dot_general(qd_scr[s, 0], kd_scr[s, 0], trans_b,
                                     preferred_element_type=F32)
            for idx, w in reversed(list(enumerate(levels))):
                mixed = qd_scr[s, idx + 1]
                p = lax.dot_general(mixed, mixed, trans_b, preferred_element_type=F32)
                scores = jnp.where(split >= w, p, scores)
            sc_scr[h % 2] = jnp.where(row >= col, scores, 0.0).astype(BF16)

        def outputs(h):
            s = h % REC_STAGES
            v = v_ref[rows, head_cols(h)]
            st = state_ref[h]
            o = (lax.dot_general(qd_scr[s, n_lv + 1], st.astype(BF16), trans_b,
                                 preferred_element_type=F32)
                 + jnp.dot(sc_scr[h % 2], v, preferred_element_type=F32))
            upd = lax.dot_general(v, kd_scr[s, 1], (((0,), (0,)), ((), ())),
                                  preferred_element_type=F32)
            state_ref[h] = st * jnp.exp(b_scr[h, t - 1:t]) + upd
            o = _rms(o, gw_ref[...]) * g_ref[rows, head_cols(h)]
            o_ref[rows, head_cols(h)] = o.astype(o_ref.dtype)

        for h in range(heads):
            cumulative_decay(h)
        for h in range(min(2, heads)):
            operands(h)
        block_scores(0)
        for h in range(heads):
            if h + 2 < heads:
                operands(h + 2)
            if h + 1 < heads:
                block_scores(h + 1)
            outputs(h)
        return carry

    lax.fori_loop(0, q_ref.shape[0] // t, token_block, 0)


def _hgrn2(rq, lf, kk, ri, rg, gnorm_w, batch, seq):
    n, width = rq.shape
    heads = width // HEAD_WIDTH
    t = min(REC_BLOCK, seq)
    assert seq % t == 0 and t & (t - 1) == 0 and t >= 16
    ts = min(REC_STEP_TOKENS, seq)
    assert seq % ts == 0 and ts % t == 0
    nb = seq // ts
    tri =jnp.asarray(np.tril(np.ones((t, t), np.float32)), BF16)
    n_lv = len(_rec_levels(t))
    blk = pl.BlockSpec((ts, width), lambda b, c: (b * nb + c, 0))
    return pl.pallas_call(
        functools.partial(_rec_kernel, heads=heads, t=t),
        out_shape=jax.ShapeDtypeStruct((n, width), BF16),
        grid=(batch, nb),
        in_specs=[pl.BlockSpec((t, t), lambda b, c: (0, 0)), blk, blk, blk, blk, blk,
                  pl.BlockSpec((1, HEAD_WIDTH), lambda b, c: (0, 0))],
        out_specs=blk,
        scratch_shapes=[pltpu.VMEM((heads, HEAD_WIDTH, HEAD_WIDTH), F32),
                        pltpu.VMEM((heads, t, HEAD_WIDTH), F32),
                        pltpu.VMEM((REC_STAGES, n_lv + 2, t, HEAD_WIDTH), BF16),
                        pltpu.VMEM((REC_STAGES, 2, t, HEAD_WIDTH), BF16),
                        pltpu.VMEM((2, t, t), BF16)],
        compiler_params=_params(("parallel", "arbitrary")),
        name="hgrn2",
    )(tri, rq, lf, kk, ri, rg, gnorm_w.reshape(1, -1))


def _out_kernel(x_ref, att_ref, rec_ref, wa_ref, wr_ref, o_ref):
    o_ref[...] = (x_ref[...]
                  + jnp.dot(att_ref[...], wa_ref[...], preferred_element_type=F32)
                  + jnp.dot(rec_ref[...], wr_ref[...], preferred_element_type=F32))


def _out_proj(x, att, rec, w_out, layer):
    n, d = x.shape
    wa = att.shape[1]
    wr = rec.shape[1]
    tm = min(OUT_TOKENS, n)
    assert n % tm == 0
    row = lambda i: (i, 0)
    return pl.pallas_call(
        _out_kernel,
        out_shape=jax.ShapeDtypeStruct((n, d), F32),
        grid=(n // tm,),
        in_specs=[pl.BlockSpec((tm, d), row), pl.BlockSpec((tm, wa), row),
                  pl.BlockSpec((tm, wr), row),
                  pl.BlockSpec((None, None, wa, d), lambda i: (layer, 0, 0, 0)),
                  pl.BlockSpec((None, None, wr, d), lambda i: (layer, 0, 1, 0))],
        out_specs=pl.BlockSpec((tm, d), row),
        compiler_params=_params(("parallel",)),
        name="out_proj",
    )(x, att, rec, w_out, w_out)


def kernel(x, ffn1_norm, ffn1_w_gate, ffn1_w_up, ffn1_w_down, mix_norm, w_in, w_out, lambda_q1, lambda_k1, lambda_q2, lambda_k2, diff_subln, hgrn_lower_bounds, hgrn_gnorm, ffn2_norm, ffn2_w_gate, ffn2_w_up, ffn2_w_down, final_norm):
    batch, seq, d = x.shape
    depth = w_in.shape[0]
    assert w_out.shape[1] == 2 * w_in.shape[2] // N_IN_PARTS

    tables = _rope_tables(seq)
    lb_raw = hgrn_lower_bounds.astype(F32)
    tf = min(FFN_HIDDEN_BLOCK, ffn1_w_gate.shape[2])
    f1 = (_to_bf16_blocks(ffn1_w_gate, tf), _to_bf16_blocks(ffn1_w_up, tf),
          _to_bf16_blocks(ffn1_w_down))
    f2 = (_to_bf16_blocks(ffn2_w_gate, tf), _to_bf16_blocks(ffn2_w_up, tf),
          _to_bf16_blocks(ffn2_w_down))
    w_in_b = _to_bf16_blocks(w_in, w_in.shape[2] // N_IN_PARTS)
    w_out_b = _to_bf16_blocks(w_out)

    h = x.reshape(batch * seq, d)
    for l in range(depth):
        lambda_init = 0.8 - 0.6 * math.exp(-0.3 * l)
        h = _ffn(h, ffn1_norm[l], *f1, l)
        q, k, vt, rq, lf, kk, ri, rg = _proj(h, mix_norm[l], w_in_b, tables, lb_raw, l, seq)
        att = _attention(q, k, vt, lambda_q1[l], lambda_k1[l], lambda_q2[l], lambda_k2[l],
                         diff_subln[l], batch, seq, lambda_init)
        rec = _hgrn2(rq, lf, kk, ri, rg, hgrn_gnorm[l], batch, seq)
        h = _out_proj(h, att, rec, w_out_b, l)
        h = _ffn(h, ffn2_norm[l], *f2, l, final_w=final_norm if l == depth - 1 else None)
    return h.reshape(batch, seq, d)
```

```python
import functools
import math

import jax
import jax.numpy as jnp
import numpy as np
from jax import lax
from jax.experimental import pallas as pl
from jax.experimental.pallas import tpu as pltpu

NORM_EPS = 1e-6
ROPE_THETA = 10000.0
ATT_HEAD_DIM = 64
HEAD_WIDTH = 128
N_IN_PARTS = 7
LANES = 128
V7X_VMEM_LIMIT_BYTES = 60 * 1024 * 1024
CAST_BLOCK_BYTES = 6 * 1024 * 1024

FFN_TOKENS = 1024
FFN_HIDDEN_BLOCK = 512
PROJ_TOKENS = 256
OUT_TOKENS = 1024
ATT_Q_BLOCK = 512
ATT_KV_BLOCK = 512
ATT_SUM_ROWS = 16
ATT_STAGES = 3
REC_BLOCK = 128
REC_STAGES = 3
REC_STEP_TOKENS = 512

F32 = jnp.float32
BF16 = jnp.bfloat16


def _rms(x, w):
    return x * lax.rsqrt(jnp.mean(x * x, axis=-1, keepdims=True) + NORM_EPS) * w


def _silu(x):
    return x * jax.nn.sigmoid(x)


def _params(semantics):
    return pltpu.CompilerParams(dimension_semantics=semantics,
                                vmem_limit_bytes=V7X_VMEM_LIMIT_BYTES)


def _cast_kernel(w_ref, o_ref, *, cols):
    block = w_ref.shape[1]
    w = w_ref[...]
    if cols % block:
        lane = lax.broadcasted_iota(jnp.int32, w.shape, 1)
        w = jnp.where(pl.program_id(1) * block + lane < cols, w, 0.0)
    o_ref[...] = w.astype(o_ref.dtype)


def _to_bf16_blocks(w, col_block=None):
    depth, rows, cols = w.shape
    col_block = cols if col_block is None else col_block
    n_blocks = pl.cdiv(cols, col_block)
    fits = [r for r in range(16, rows + 1, 16)
            if rows % r == 0 and r * col_block * w.dtype.itemsize <= CAST_BLOCK_BYTES]
    rb = max(fits)
    return pl.pallas_call(
        functools.partial(_cast_kernel, cols=cols),
        out_shape=jax.ShapeDtypeStruct((depth, n_blocks, rows, col_block), BF16),
        grid=(depth, n_blocks, rows // rb),
        in_specs=[pl.BlockSpec((None, rb, col_block), lambda l, c, r: (l, r, c))],
        out_specs=pl.BlockSpec((None, None, rb, col_block), lambda l, c, r: (l, c, r, 0)),
        compiler_params=_params(("parallel", "parallel", "parallel")),
        name="to_bf16",
    )(w)


def _ffn_kernel(x_ref, nw_ref, wg_ref, wu_ref, wd_ref, *rest, final, tail):
    if final:
        fw_ref, o_ref, hn_ref = rest
    else:
        o_ref, hn_ref = rest
    j = pl.program_id(1)
    last = pl.num_programs(1) - 1
    tf = wg_ref.shape[1]

    @pl.when(j == 0)
    def _():
        x = x_ref[...]
        hn_ref[...] = _rms(x, nw_ref[...]).astype(BF16)
        o_ref[...] = x

    def hidden_block(width):
        h = hn_ref[...]
        g = jnp.dot(h, wg_ref[:, :width], preferred_element_type=F32)
        u = jnp.dot(h, wu_ref[:, :width], preferred_element_type=F32)
        a = (0.5 * _silu(g) * u).astype(BF16)
        o_ref[...] += jnp.dot(a, wd_ref[:width, :], preferred_element_type=F32)

    if tail == tf:
        hidden_block(tf)
    else:
        pl.when(j == 0)(functools.partial(hidden_block, tail))
        pl.when(j > 0)(functools.partial(hidden_block, tf))

    if final:
        @pl.when(j == last)
        def _():
            o_ref[...] = _rms(o_ref[...], fw_ref[...])


def _ffn(x, norm_w, wg, wu, wd, layer, final_w=None):
    n, d = x.shape
    f = wd.shape[2]
    n_hidden, tf = wg.shape[1], wg.shape[3]
    tm = min(FFN_TOKENS, n)
    assert n % tm == 0 and n_hidden == pl.cdiv(f, tf)
    tail = f - (n_hidden - 1) * tf
    final = final_w is not None
    row = lambda i, j: (i, 0)
    vec = pl.BlockSpec((1, d), lambda i, j: (0, 0))
    hidden = (lambda j: j) if tail == tf else (lambda j: (j + n_hidden - 1) % n_hidden)
    in_specs = [pl.BlockSpec((tm, d), row), vec,
                pl.BlockSpec((None, None, d, tf), lambda i, j: (layer, hidden(j), 0, 0)),
                pl.BlockSpec((None, None, d, tf), lambda i, j: (layer, hidden(j), 0, 0)),
                pl.BlockSpec((None, None, tf, d), lambda i, j: (layer, 0, hidden(j), 0))]
    args = [x, norm_w.reshape(1, d), wg, wu, wd]
    if final:
        in_specs.append(vec)
        args.append(final_w.reshape(1, d))
    return pl.pallas_call(
        functools.partial(_ffn_kernel, final=final, tail=tail),
        out_shape=jax.ShapeDtypeStruct((n, d), F32),
        grid=(n // tm, n_hidden),
        in_specs=in_specs,
        out_specs=pl.BlockSpec((tm, d), row),
        scratch_shapes=[pltpu.VMEM((tm, d), BF16)],
        compiler_params=_params(("parallel", "arbitrary")),
        name="ffn_final" if final else "ffn",
    )(*args)


def _rope(y, cos, sin_lo, sin_hi):
    width = y.shape[1]
    from_hi = pltpu.roll(y, width - ATT_HEAD_DIM // 2, 1)
    from_lo = pltpu.roll(y, ATT_HEAD_DIM // 2, 1)
    reps = width // LANES
    tile = lambda t: jnp.concatenate([t] * reps, axis=1)
    return y * tile(cos) + from_hi * tile(sin_lo) + from_lo * tile(sin_hi)


def _proj_kernel(x_ref, nw_ref, w_ref, cos_ref, slo_ref, shi_ref, lbraw_ref,
                 q_o, k_o, vt_o, rq_o, lf_o, kk_o, ri_o, rg_o, hn_ref, y_scr, *, layer):
    hn_ref[...] = _rms(x_ref[...], nw_ref[...]).astype(BF16)

    def finish(part, y):
        if part == 0:
            scale = ATT_HEAD_DIM ** -0.5 * math.log2(math.e)
            q_o[...] = (_rope(y, cos_ref[...], slo_ref[...], shi_ref[...]) * scale).astype(BF16)
        elif part == 1:
            k_o[...] = _rope(y, cos_ref[...], slo_ref[...], shi_ref[...]).astype(BF16)
        elif part == 2:
            vt_o[...] = y.T.astype(BF16)
        elif part == 3:
            rq_o[...] = _silu(y)
        elif part == 4:
            raw = lbraw_ref[...]
            e = jnp.exp(raw - jnp.max(raw, axis=0, keepdims=True))
            sm = e / jnp.sum(e, axis=0, keepdims=True)
            lb = jnp.sum(sm[:layer + 1], axis=0, keepdims=True) - sm[0:1]
            lf_o[...] = jnp.log(lb + (1.0 - lb) * jax.nn.sigmoid(y))
            kk_o[...] = (1.0 - lb) * jax.nn.sigmoid(-y)
        elif part == 5:
            ri_o[...] = y.astype(BF16)
        else:
            rg_o[...] = _silu(y)

    for s in range(N_IN_PARTS + 1):
        if s >= 1:
            finish(s - 1, y_scr[(s - 1) % 2])
        if s < N_IN_PARTS:
            y_scr[s % 2] = jnp.dot(hn_ref[...], w_ref[s], preferred_element_type=F32)


def _proj(x, norm_w, w_in, tables, lb_raw, layer, seq):
    n, d = x.shape
    width = w_in.shape[3]
    assert w_in.shape[1] == N_IN_PARTS
    tm = min(PROJ_TOKENS, seq)
    assert n % tm == 0 and seq % tm == 0
    pos_blocks = seq // tm
    depth = lb_raw.shape[0]
    row = lambda i: (i, 0)
    tab = pl.BlockSpec((tm, LANES), lambda i: (i % pos_blocks, 0))
    out_dtypes = (BF16, BF16, BF16, F32, F32, F32, BF16, F32)
    return pl.pallas_call(
        functools.partial(_proj_kernel, layer=layer),
        out_shape=[jax.ShapeDtypeStruct((width, n) if o == 2 else (n, width), dt)
                   for o, dt in enumerate(out_dtypes)],
        grid=(n // tm,),
        in_specs=[pl.BlockSpec((tm, d), row),
                  pl.BlockSpec((1, d), lambda i: (0, 0)),
                  pl.BlockSpec((None, N_IN_PARTS, d, width), lambda i: (layer, 0, 0, 0),
                               pipeline_mode=pl.Buffered(1)),
                  tab, tab, tab,
                  pl.BlockSpec((depth, width), lambda i: (0, 0))],
        out_specs=[pl.BlockSpec((width, tm), lambda i: (0, i)) if o == 2
                   else pl.BlockSpec((tm, width), row) for o in range(len(out_dtypes))],
        scratch_shapes=[pltpu.VMEM((tm, d), BF16), pltpu.VMEM((2, tm, width), F32)],
        compiler_params=_params(("parallel",)),
        name="in_proj",
    )(x, norm_w.reshape(1, d), w_in, *tables, lb_raw)


def _rope_tables(seq):
    inv_freq = ROPE_THETA ** (-jnp.arange(0, ATT_HEAD_DIM, 2, dtype=F32) / ATT_HEAD_DIM)
    ang = jnp.arange(seq, dtype=F32)[:, None] * inv_freq[None, :]
    cos, sin, zero = jnp.cos(ang), jnp.sin(ang), jnp.zeros_like(ang)
    reps = LANES // ATT_HEAD_DIM
    cos_t = jnp.tile(jnp.concatenate([cos, cos], axis=1), (1, reps))
    sin_lo = jnp.tile(jnp.concatenate([-sin, zero], axis=1), (1, reps))
    sin_hi = jnp.tile(jnp.concatenate([zero, sin], axis=1), (1, reps))
    return cos_t, sin_lo, sin_hi


def _att_schedule(seq, tq, tk):
    qi, kj, first, last, diag = [], [], [], [], []
    for i in range(seq // tq):
        jmax = ((i + 1) * tq - 1) // tk
        for j in range(jmax + 1):
            qi.append(i)
            kj.append(j)
            first.append(int(j == 0))
            last.append(int(j == jmax))
            diag.append(int((j + 1) * tk - 1 > i * tq))
    return [np.asarray(a, np.int32) for a in (qi, kj, first, last, diag)]


def _att_kernel(qi_ref, kj_ref, first_ref, last_ref, diag_ref,
                q_ref, k_ref, vt_ref, par_ref,
                o_ref, qz_ref, m_ref, acc_ref, s_scr, p_scr, *, heads, tq, tk, lambda_init):
    t = pl.program_id(1)
    qi, kj = qi_ref[t], kj_ref[t]

    @pl.when(first_ref[t] == 1)
    def _():
        m_ref[...] = jnp.full(m_ref.shape, -jnp.inf, F32)
        acc_ref[...] = jnp.zeros(acc_ref.shape, F32)
        q = q_ref[...]
        low = lax.broadcasted_iota(jnp.int32, q.shape, 1) % HEAD_WIDTH < ATT_HEAD_DIM
        zero = jnp.zeros_like(q)
        qz_ref[0:tq, :] = jnp.where(low, q, zero)
        qz_ref[tq:2 * tq, :] = jnp.where(low, zero, q)

    ones_rows = jnp.ones((ATT_SUM_ROWS, tk), BF16)

    def step(masked):
        if masked:
            row = lax.broadcasted_iota(jnp.int32, (tk, 2 * tq), 0)
            col = lax.broadcasted_iota(jnp.int32, (tk, 2 * tq), 1)
            qpos = qi * tq + jnp.where(col >= tq, col - tq, col)
            keep = kj * tk + row <= qpos

        def scores(h):
            cols = slice(h * HEAD_WIDTH, (h + 1) * HEAD_WIDTH)
            s = lax.dot_general(k_ref[:, cols], qz_ref[:, cols], (((1,), (1,)), ((), ())),
                                preferred_element_type=F32)
            if masked:
                s = jnp.where(keep, s, -jnp.inf)
            s_scr[h % ATT_STAGES] = s
            return jnp.max(s, axis=0, keepdims=True)

        def probs(h, m_cur):
            m_prev = m_ref[h]
            m_new = jnp.maximum(m_prev, m_cur)
            p_scr[h % ATT_STAGES] = jnp.exp2(s_scr[h % ATT_STAGES] - m_new).astype(BF16)
            m_ref[h] = m_new
            return jnp.exp2(m_prev - m_new)

        def values(h, alpha):
            cols = slice(h * HEAD_WIDTH, (h + 1) * HEAD_WIDTH)
            vt_ext = jnp.concatenate([vt_ref[cols, :], ones_rows], axis=0)
            acc_ref[h] = alpha * acc_ref[h] + jnp.dot(vt_ext, p_scr[h % ATT_STAGES],
                                                      preferred_element_type=F32)

        ahead = ATT_STAGES - 1
        m_cur = {h: scores(h) for h in range(min(ahead, heads))}
        for h in range(heads):
            if h + ahead < heads:
                m_cur[h + ahead] = scores(h + ahead)
            values(h, probs(h, m_cur.pop(h)))

    @pl.when(diag_ref[t] == 1)
    def _():
        step(True)

    @pl.when(diag_ref[t] == 0)
    def _():
        step(False)

    @pl.when(last_ref[t] == 1)
    def _():
        par = par_ref[...]
        lam = (jnp.exp(jnp.sum(par[0:1] * par[1:2], axis=1, keepdims=True))
               - jnp.exp(jnp.sum(par[2:3] * par[3:4], axis=1, keepdims=True))
               + lambda_init)
        for h in range(heads):
            a = acc_ref[h]
            o12 = a[0:HEAD_WIDTH] / a[HEAD_WIDTH:HEAD_WIDTH + 1]
            o = (o12[:, 0:tq] - lam * o12[:, tq:2 * tq]).T
            o = _rms(o, par[4:5]) * (1.0 - lambda_init)
            o_ref[:, h * HEAD_WIDTH:(h + 1) * HEAD_WIDTH] = o.astype(o_ref.dtype)


def _attention(q, k, vt, lq1, lk1, lq2, lk2, subln_w, batch, seq, lambda_init):
    n, width = q.shape
    heads = width // HEAD_WIDTH
    tq = min(ATT_Q_BLOCK, seq)
    tk = min(ATT_KV_BLOCK, seq)
    assert seq % tq == 0 and seq % tk == 0
    sched = _att_schedule(seq, tq, tk)
    nq, nk = seq // tq, seq // tk
    rows = [jnp.pad(v.astype(F32), (0, HEAD_WIDTH - v.shape[0])) for v in (lq1, lk1, lq2, lk2, subln_w)]
    par = jnp.stack(rows + [jnp.zeros((HEAD_WIDTH,), F32)] * (8 - len(rows)))
    grid_spec = pltpu.PrefetchScalarGridSpec(
        num_scalar_prefetch=len(sched),
        grid=(batch, len(sched[0])),
        in_specs=[pl.BlockSpec((tq, width), lambda b, t, qi, kj, *_: (b * nq + qi[t], 0)),
                  pl.BlockSpec((tk, width), lambda b, t, qi, kj, *_: (b * nk + kj[t], 0)),
                  pl.BlockSpec((width, tk), lambda b, t, qi, kj, *_: (0, b * nk + kj[t])),
                  pl.BlockSpec(par.shape, lambda b, t, *_: (0, 0))],
        out_specs=pl.BlockSpec((tq, width), lambda b, t, qi, kj, *_: (b * nq + qi[t], 0)),
        scratch_shapes=[pltpu.VMEM((2 * tq, width), BF16),
                        pltpu.VMEM((heads, 1, 2 * tq), F32),
                        pltpu.VMEM((heads, HEAD_WIDTH + ATT_SUM_ROWS, 2 * tq), F32),
                        pltpu.VMEM((ATT_STAGES, tk, 2 * tq), F32),
                        pltpu.VMEM((ATT_STAGES, tk, 2 * tq), BF16)],
    )
    return pl.pallas_call(
        functools.partial(_att_kernel, heads=heads, tq=tq, tk=tk, lambda_init=lambda_init),
        out_shape=jax.ShapeDtypeStruct((n, width), BF16),
        grid_spec=grid_spec,
        compiler_params=_params(("parallel", "arbitrary")),
        name="diff_attention",
    )(*sched, q, k, vt, par)


def _rec_levels(t):
    return [t >> s for s in range(1, t.bit_length())]


def _rec_kernel(tri_ref, q_ref, lf_ref, kk_ref, v_ref, g_ref, gw_ref, o_ref, state_ref,
                b_scr, qd_scr, kd_scr, sc_scr, *, heads, t):
    @pl.when(pl.program_id(1) == 0)
    def _():
        state_ref[...] = jnp.zeros(state_ref.shape, F32)

    levels = _rec_levels(t)
    n_lv = len(levels)
    tok = lax.broadcasted_iota(jnp.int32, (t, HEAD_WIDTH), 0)
    row = lax.broadcasted_iota(jnp.int32, (t, t), 0)
    col = lax.broadcasted_iota(jnp.int32, (t, t), 1)
    split = row ^ col
    trans_b = (((1,), (1,)), ((), ()))
    head_cols = lambda h: slice(h * HEAD_WIDTH, (h + 1) * HEAD_WIDTH)

    def token_block(c, carry):
        rows = pl.ds(pl.multiple_of(c * t, t), t)

        def cumulative_decay(h):
            lf = lf_ref[rows, head_cols(h)]
            lf_hi = lf.astype(BF16)
            lf_lo = (lf - lf_hi.astype(F32)).astype(BF16)
            b2 = jnp.dot(tri_ref[...], jnp.concatenate([lf_hi, lf_lo], axis=1),
                         preferred_element_type=F32)
            b_scr[h] = b2[:, :HEAD_WIDTH] + b2[:, HEAD_WIDTH:]

        def level_exponent(h, w, lf):
            late = (tok & w) != 0
            if w == 1:
                return jnp.where(late, lf, 0.0)
            if w == 2:
                nxt = pltpu.roll(lf, t - 1, 0)
                prv = pltpu.roll(lf, 1, 0)
                odd = (tok & 1) != 0
                return jnp.where(late, jnp.where(odd, lf + prv, lf), jnp.where(odd, 0.0, nxt))
            d = jnp.concatenate([b_scr[h, r0:r0 + 2 * w] - b_scr[h, r0 + w - 1:r0 + w]
                                 for r0 in range(0, t, 2 * w)], axis=0)
            return jnp.where(late, d, -d)

        def operands(h):
            q, kk = q_ref[rows, head_cols(h)], kk_ref[rows, head_cols(h)]
            lf = lf_ref[rows, head_cols(h)]
            s = h % REC_STAGES
            qd_scr[s, 0] = q.astype(BF16)
            kd_scr[s, 0] = kk.astype(BF16)
            for idx, w in enumerate(levels):
                late = (tok & w) != 0
                mixed = jnp.where(late, q, kk) * jnp.exp(level_exponent(h, w, lf))
                qd_scr[s, idx + 1] = mixed.astype(BF16)
            b = b_scr[h]
            qd_scr[s, n_lv + 1] = (q * jnp.exp(b)).astype(BF16)
            kd_scr[s, 1] = (kk * jnp.exp(b_scr[h, t - 1:t] - b)).astype(BF16)

        def block_scores(h):
            s = h % REC_STAGES
            scores = lax.dot_general(qd_scr[s, 0], kd_scr[s, 0], trans_b,
                                     preferred_element_type=F32)
            for idx, w in reversed(list(enumerate(levels))):
                mixed = qd_scr[s, idx + 1]
                p = lax.dot_general(mixed, mixed, trans_b, preferred_element_type=F32)
                scores = jnp.where(split >= w, p, scores)
            sc_scr[h % 2] = jnp.where(row >= col, scores, 0.0).astype(BF16)

        def outputs(h):
            s = h % REC_STAGES
            v = v_ref[rows, head_cols(h)]
            st = state_ref[h]
            o = (lax.dot_general(qd_scr[s, n_lv + 1], st.astype(BF16), trans_b,
                                 preferred_element_type=F32)
                 + jnp.dot(sc_scr[h % 2], v, preferred_element_type=F32))
            upd = lax.dot_general(v, kd_scr[s, 1], (((0,), (0,)), ((), ())),
                                  preferred_element_type=F32)
            state_ref[h] = st * jnp.exp(b_scr[h, t - 1:t]) + upd
            o = _rms(o, gw_ref[...]) * g_ref[rows, head_cols(h)]
            o_ref[rows, head_cols(h)] = o.astype(o_ref.dtype)

        for h in range(heads):
            cumulative_decay(h)
        for h in range(min(2, heads)):
            operands(h)
        block_scores(0)
        for h in range(heads):
            if h + 2 < heads:
                operands(h + 2)
            if h + 1 < heads:
                block_scores(h + 1)
            outputs(h)
        return carry

    lax.fori_loop(0, q_ref.shape[0] // t, token_block, 0)


def _hgrn2(rq, lf, kk, ri, rg, gnorm_w, batch, seq):
    n, width = rq.shape
    heads = width // HEAD_WIDTH
    t = min(REC_BLOCK, seq)
    assert seq % t == 0 and t & (t - 1) == 0 and t >= 16
    ts = min(REC_STEP_TOKENS, seq)
    assert seq % ts == 0 and ts % t == 0
    nb = seq // ts
    tri =jnp.asarray(np.tril(np.ones((t, t), np.float32)), BF16)
    n_lv = len(_rec_levels(t))
    blk = pl.BlockSpec((ts, width), lambda b, c: (b * nb + c, 0))
    return pl.pallas_call(
        functools.partial(_rec_kernel, heads=heads, t=t),
        out_shape=jax.ShapeDtypeStruct((n, width), BF16),
        grid=(batch, nb),
        in_specs=[pl.BlockSpec((t, t), lambda b, c: (0, 0)), blk, blk, blk, blk, blk,
                  pl.BlockSpec((1, HEAD_WIDTH), lambda b, c: (0, 0))],
        out_specs=blk,
        scratch_shapes=[pltpu.VMEM((heads, HEAD_WIDTH, HEAD_WIDTH), F32),
                        pltpu.VMEM((heads, t, HEAD_WIDTH), F32),
                        pltpu.VMEM((REC_STAGES, n_lv + 2, t, HEAD_WIDTH), BF16),
                        pltpu.VMEM((REC_STAGES, 2, t, HEAD_WIDTH), BF16),
                        pltpu.VMEM((2, t, t), BF16)],
        compiler_params=_params(("parallel", "arbitrary")),
        name="hgrn2",
    )(tri, rq, lf, kk, ri, rg, gnorm_w.reshape(1, -1))


def _out_kernel(x_ref, att_ref, rec_ref, wa_ref, wr_ref, o_ref):
    o_ref[...] = (x_ref[...]
                  + jnp.dot(att_ref[...], wa_ref[...], preferred_element_type=F32)
                  + jnp.dot(rec_ref[...], wr_ref[...], preferred_element_type=F32))


def _out_proj(x, att, rec, w_out, layer):
    n, d = x.shape
    wa = att.shape[1]
    wr = rec.shape[1]
    tm = min(OUT_TOKENS, n)
    assert n % tm == 0
    row = lambda i: (i, 0)
    return pl.pallas_call(
        _out_kernel,
        out_shape=jax.ShapeDtypeStruct((n, d), F32),
        grid=(n // tm,),
        in_specs=[pl.BlockSpec((tm, d), row), pl.BlockSpec((tm, wa), row),
                  pl.BlockSpec((tm, wr), row),
                  pl.BlockSpec((None, None, wa, d), lambda i: (layer, 0, 0, 0),
                               pipeline_mode=pl.Buffered(1)),
                  pl.BlockSpec((None, None, wr, d), lambda i: (layer, 0, 1, 0),
                               pipeline_mode=pl.Buffered(1))],
        out_specs=pl.BlockSpec((tm, d), row),
        compiler_params=_params(("parallel",)),
        name="out_proj",
    )(x, att, rec, w_out, w_out)


def kernel(x, ffn1_norm, ffn1_w_gate, ffn1_w_up, ffn1_w_down, mix_norm, w_in, w_out, lambda_q1, lambda_k1, lambda_q2, lambda_k2, diff_subln, hgrn_lower_bounds, hgrn_gnorm, ffn2_norm, ffn2_w_gate, ffn2_w_up, ffn2_w_down, final_norm):
    batch, seq, d = x.shape
    depth = w_in.shape[0]
    assert w_out.shape[1] == 2 * w_in.shape[2] // N_IN_PARTS

    tables = _rope_tables(seq)
    lb_raw = hgrn_lower_bounds.astype(F32)
    tf = min(FFN_HIDDEN_BLOCK, ffn1_w_gate.shape[2])
    f1 = (_to_bf16_blocks(ffn1_w_gate, tf), _to_bf16_blocks(ffn1_w_up, tf),
          _to_bf16_blocks(ffn1_w_down))
    f2 = (_to_bf16_blocks(ffn2_w_gate, tf), _to_bf16_blocks(ffn2_w_up, tf),
          _to_bf16_blocks(ffn2_w_down))
    w_in_b = _to_bf16_blocks(w_in, w_in.shape[2] // N_IN_PARTS)
    w_out_b = _to_bf16_blocks(w_out)

    h = x.reshape(batch * seq, d)
    for l in range(depth):
        lambda_init = 0.8 - 0.6 * math.exp(-0.3 * l)
        h = _ffn(h, ffn1_norm[l], *f1, l)
        q, k, vt, rq, lf, kk, ri, rg = _proj(h, mix_norm[l], w_in_b, tables, lb_raw, l, seq)
        att = _attention(q, k, vt, lambda_q1[l], lambda_k1[l], lambda_q2[l], lambda_k2[l],
                         diff_subln[l], batch, seq, lambda_init)
        rec = _hgrn2(rq, lf, kk, ri, rg, hgrn_gnorm[l], batch, seq)
        h = _out_proj(h, att, rec, w_out_b, l)
        h = _ffn(h, ffn2_norm[l], *f2, l, final_w=final_norm if l == depth - 1 else None)
    return h.reshape(batch, seq, d)
```

```python
import functools
import math

import jax
import jax.numpy as jnp
import numpy as np
from jax import lax
from jax.experimental import pallas as pl
from jax.experimental.pallas import tpu as pltpu

NORM_EPS = 1e-6
ROPE_THETA = 10000.0
ATT_HEAD_DIM = 64
HEAD_WIDTH = 128
N_IN_PARTS = 7
LANES = 128
V7X_VMEM_LIMIT_BYTES = 62 * 1024 * 1024
CAST_BLOCK_BYTES = 6 * 1024 * 1024

FFN_TOKENS = 1024
FFN_HIDDEN_BLOCK = 512
PROJ_TOKENS = 256
OUT_TOKENS = 512
ATT_Q_BLOCK = 512
ATT_SUM_ROWS = 16
ATT_STAGES = 3
REC_BLOCK = 128
REC_STAGES = 3
REC_STEP_TOKENS = 512

F32 = jnp.float32
BF16 = jnp.bfloat16


def _rms(x, w):
    return x * lax.rsqrt(jnp.mean(x * x, axis=-1, keepdims=True) + NORM_EPS) * w


def _silu(x):
    return x * jax.nn.sigmoid(x)


def _params(semantics):
    return pltpu.CompilerParams(dimension_semantics=semantics,
                                vmem_limit_bytes=V7X_VMEM_LIMIT_BYTES)


def _cast_kernel(w_ref, o_ref, *, cols):
    block = w_ref.shape[1]
    w = w_ref[...]
    if cols % block:
        lane = lax.broadcasted_iota(jnp.int32, w.shape, 1)
        w = jnp.where(pl.program_id(1) * block + lane < cols, w, 0.0)
    o_ref[...] = w.astype(o_ref.dtype)


def _to_bf16_blocks(w, col_block=None):
    depth, rows, cols = w.shape
    col_block = cols if col_block is None else col_block
    n_blocks = pl.cdiv(cols, col_block)
    fits = [r for r in range(16, rows + 1, 16)
            if rows % r == 0 and r * col_block * w.dtype.itemsize <= CAST_BLOCK_BYTES]
    rb = max(fits)
    return pl.pallas_call(
        functools.partial(_cast_kernel, cols=cols),
        out_shape=jax.ShapeDtypeStruct((depth, n_blocks, rows, col_block), BF16),
        grid=(depth, n_blocks, rows // rb),
        in_specs=[pl.BlockSpec((None, rb, col_block), lambda l, c, r: (l, r, c))],
        out_specs=pl.BlockSpec((None, None, rb, col_block), lambda l, c, r: (l, c, r, 0)),
        compiler_params=_params(("parallel", "parallel", "parallel")),
        name="to_bf16",
    )(w)


def _ffn_kernel(x_ref, nw_ref, wg_ref, wu_ref, wd_ref, *rest, final, tail):
    if final:
        fw_ref, o_ref, hn_ref = rest
    else:
        o_ref, hn_ref = rest
    j = pl.program_id(1)
    last = pl.num_programs(1) - 1
    tf = wg_ref.shape[1]

    @pl.when(j == 0)
    def _():
        x = x_ref[...]
        hn_ref[...] = _rms(x, nw_ref[...]).astype(BF16)
        o_ref[...] = x

    def hidden_block(width):
        h = hn_ref[...]
        g = jnp.dot(h, wg_ref[:, :width], preferred_element_type=F32)
        u = jnp.dot(h, wu_ref[:, :width], preferred_element_type=F32)
        a = (0.5 * _silu(g) * u).astype(BF16)
        o_ref[...] += jnp.dot(a, wd_ref[:width, :], preferred_element_type=F32)

    if tail == tf:
        hidden_block(tf)
    else:
        pl.when(j == 0)(functools.partial(hidden_block, tail))
        pl.when(j > 0)(functools.partial(hidden_block, tf))

    if final:
        @pl.when(j == last)
        def _():
            o_ref[...] = _rms(o_ref[...], fw_ref[...])


def _ffn(x, norm_w, wg, wu, wd, layer, final_w=None):
    n, d = x.shape
    f = wd.shape[2]
    n_hidden, tf = wg.shape[1], wg.shape[3]
    tm = min(FFN_TOKENS, n)
    assert n % tm == 0 and n_hidden == pl.cdiv(f, tf)
    tail = f - (n_hidden - 1) * tf
    final = final_w is not None
    row = lambda i, j: (i, 0)
    vec = pl.BlockSpec((1, d), lambda i, j: (0, 0))
    hidden = (lambda j: j) if tail == tf else (lambda j: (j + n_hidden - 1) % n_hidden)
    in_specs = [pl.BlockSpec((tm, d), row), vec,
                pl.BlockSpec((None, None, d, tf), lambda i, j: (layer, hidden(j), 0, 0)),
                pl.BlockSpec((None, None, d, tf), lambda i, j: (layer, hidden(j), 0, 0)),
                pl.BlockSpec((None, None, tf, d), lambda i, j: (layer, 0, hidden(j), 0))]
    args = [x, norm_w.reshape(1, d), wg, wu, wd]
    if final:
        in_specs.append(vec)
        args.append(final_w.reshape(1, d))
    return pl.pallas_call(
        functools.partial(_ffn_kernel, final=final, tail=tail),
        out_shape=jax.ShapeDtypeStruct((n, d), F32),
        grid=(n // tm, n_hidden),
        in_specs=in_specs,
        out_specs=pl.BlockSpec((tm, d), row),
        scratch_shapes=[pltpu.VMEM((tm, d), BF16)],
        compiler_params=_params(("parallel", "arbitrary")),
        name="ffn_final" if final else "ffn",
    )(*args)


def _rope(y, cos, sin_lo, sin_hi):
    width = y.shape[1]
    from_hi = pltpu.roll(y, width - ATT_HEAD_DIM // 2, 1)
    from_lo = pltpu.roll(y, ATT_HEAD_DIM // 2, 1)
    reps = width // LANES
    tile = lambda t: jnp.concatenate([t] * reps, axis=1)
    return y * tile(cos) + from_hi * tile(sin_lo) + from_lo * tile(sin_hi)


def _proj_kernel(x_ref, nw_ref, w_ref, cos_ref, slo_ref, shi_ref, lbraw_ref,
                 q_o, k_o, vt_o, rq_o, lf_o, kk_o, ri_o, rg_o, hn_ref, y_scr, *, layer):
    hn_ref[...] = _rms(x_ref[...], nw_ref[...]).astype(BF16)

    def finish(part, y):
        if part == 0:
            scale = ATT_HEAD_DIM ** -0.5 * math.log2(math.e)
            q_o[...] = (_rope(y, cos_ref[...], slo_ref[...], shi_ref[...]) * scale).astype(BF16)
        elif part == 1:
            k_o[...] = _rope(y, cos_ref[...], slo_ref[...], shi_ref[...]).astype(BF16)
        elif part == 2:
            vt_o[...] = y.T.astype(BF16)
        elif part == 3:
            rq_o[...] = _silu(y)
        elif part == 4:
            raw = lbraw_ref[...]
            e = jnp.exp(raw - jnp.max(raw, axis=0, keepdims=True))
            sm = e / jnp.sum(e, axis=0, keepdims=True)
            lb = jnp.sum(sm[:layer + 1], axis=0, keepdims=True) - sm[0:1]
            lf_o[...] = jnp.log(lb + (1.0 - lb) * jax.nn.sigmoid(y))
            kk_o[...] = (1.0 - lb) * jax.nn.sigmoid(-y)
        elif part == 5:
            ri_o[...] = y.astype(BF16)
        else:
            rg_o[...] = _silu(y)

    for s in range(N_IN_PARTS + 1):
        if s >= 1:
            finish(s - 1, y_scr[(s - 1) % 2])
        if s < N_IN_PARTS:
            y_scr[s % 2] = jnp.dot(hn_ref[...], w_ref[s], preferred_element_type=F32)


def _proj(x, norm_w, w_in, tables, lb_raw, layer, seq):
    n, d = x.shape
    width = w_in.shape[3]
    assert w_in.shape[1] == N_IN_PARTS
    tm = min(PROJ_TOKENS, seq)
    assert n % tm == 0 and seq % tm == 0
    pos_blocks = seq // tm
    depth = lb_raw.shape[0]
    row = lambda i: (i, 0)
    tab = pl.BlockSpec((tm, LANES), lambda i: (i % pos_blocks, 0))
    out_dtypes = (BF16, BF16, BF16, F32, F32, F32, BF16, F32)
    return pl.pallas_call(
        functools.partial(_proj_kernel, layer=layer),
        out_shape=[jax.ShapeDtypeStruct((n // tm, width, tm) if o == 2 else (n, width), dt)
                   for o, dt in enumerate(out_dtypes)],
        grid=(n // tm,),
        in_specs=[pl.BlockSpec((tm, d), row),
                  pl.BlockSpec((1, d), lambda i: (0, 0)),
                  pl.BlockSpec((None, N_IN_PARTS, d, width), lambda i: (layer, 0, 0, 0),
                               pipeline_mode=pl.Buffered(1)),
                  tab, tab, tab,
                  pl.BlockSpec((depth, width), lambda i: (0, 0))],
        out_specs=[pl.BlockSpec((None, width, tm), lambda i: (i, 0, 0)) if o == 2
                   else pl.BlockSpec((tm, width), row) for o in range(len(out_dtypes))],
        scratch_shapes=[pltpu.VMEM((tm, d), BF16), pltpu.VMEM((2, tm, width), F32)],
        compiler_params=_params(("parallel",)),
        name="in_proj",
    )(x, norm_w.reshape(1, d), w_in, *tables, lb_raw)


def _rope_tables(seq):
    half = ATT_HEAD_DIM // 2
    inv_freq = ROPE_THETA ** (-jnp.arange(0, ATT_HEAD_DIM, 2, dtype=F32) / ATT_HEAD_DIM)
    ang = jnp.arange(seq, dtype=F32)[:, None] * inv_freq[None, :]
    cos, sin, zero = jnp.cos(ang), jnp.sin(ang), jnp.zeros_like(ang)
    reps = LANES // ATT_HEAD_DIM
    cos_t = jnp.tile(jnp.concatenate([cos, cos], axis=1), (1, reps))
    sin_lo = jnp.tile(jnp.concatenate([-sin, zero], axis=1), (1, reps))
    sin_hi = jnp.tile(jnp.concatenate([zero, sin], axis=1), (1, reps))
    del half
    return cos_t, sin_lo, sin_hi


def _att_kernel(q_ref, k_ref, vt_ref, par_ref, o_ref, qz_ref, m_ref, acc_ref, s_scr, p_scr,
                *, heads, tq, vt_cols, lambda_init):
    tk = tq
    qi = pl.program_id(1)

    m_ref[...] = jnp.full(m_ref.shape, -jnp.inf, F32)
    acc_ref[...] = jnp.zeros(acc_ref.shape, F32)
    q = q_ref[...]
    low = lax.broadcasted_iota(jnp.int32, q.shape, 1) % HEAD_WIDTH < ATT_HEAD_DIM
    zero = jnp.zeros_like(q)
    qz_ref[0:tq, :] = jnp.where(low, q, zero)
    qz_ref[tq:2 * tq, :] = jnp.where(low, zero, q)

    ones_rows = jnp.ones((ATT_SUM_ROWS, vt_cols), BF16)
    pieces = tk // vt_cols

    def step(kj, masked):
        rows = pl.ds(pl.multiple_of(kj * tk, tk), tk)
        if masked:
            row = lax.broadcasted_iota(jnp.int32, (tk, 2 * tq), 0)
            col = lax.broadcasted_iota(jnp.int32, (tk, 2 * tq), 1)
            keep = row <= jnp.where(col >= tq, col - tq, col)

        def scores(h):
            cols = slice(h * HEAD_WIDTH, (h + 1) * HEAD_WIDTH)
            s = lax.dot_general(k_ref[rows, cols], qz_ref[:, cols], (((1,), (1,)), ((), ())),
                                preferred_element_type=F32)
            if masked:
                s = jnp.where(keep, s, -jnp.inf)
            s_scr[h % ATT_STAGES] = s
            return jnp.max(s, axis=0, keepdims=True)

        def probs(h, m_cur):
            m_prev = m_ref[h]
            m_new = jnp.maximum(m_prev, m_cur)
            p_scr[h % ATT_STAGES] = jnp.exp2(s_scr[h % ATT_STAGES] - m_new).astype(BF16)
            m_ref[h] = m_new
            return jnp.exp2(m_prev - m_new)

        def values(h, alpha):
            cols = slice(h * HEAD_WIDTH, (h + 1) * HEAD_WIDTH)
            pv = None
            for c in range(pieces):
                vt_ext = jnp.concatenate([vt_ref[kj * pieces + c, cols, :], ones_rows], axis=0)
                part = jnp.dot(vt_ext, p_scr[h % ATT_STAGES, c * vt_cols:(c + 1) * vt_cols],
                               preferred_element_type=F32)
                pv = part if pv is None else pv + part
            acc_ref[h] = alpha * acc_ref[h] + pv

        ahead = ATT_STAGES - 1
        m_cur = {h: scores(h) for h in range(min(ahead, heads))}
        for h in range(heads):
            if h + ahead < heads:
                m_cur[h + ahead] = scores(h + ahead)
            values(h, probs(h, m_cur.pop(h)))

    def full_block(kj, carry):
        step(kj, False)
        return carry

    lax.fori_loop(0, qi, full_block, 0)
    step(qi, True)

    par = par_ref[...]
    lam = (jnp.exp(jnp.sum(par[0:1] * par[1:2], axis=1, keepdims=True))
           - jnp.exp(jnp.sum(par[2:3] * par[3:4], axis=1, keepdims=True))
           + lambda_init)
    for h in range(heads):
        a = acc_ref[h]
        o12 = a[0:HEAD_WIDTH] / a[HEAD_WIDTH:HEAD_WIDTH + 1]
        o = (o12[:, 0:tq] - lam * o12[:, tq:2 * tq]).T
        o = _rms(o, par[4:5]) * (1.0 - lambda_init)
        o_ref[:, h * HEAD_WIDTH:(h + 1) * HEAD_WIDTH] = o.astype(o_ref.dtype)


def _attention(q, k, vt, lq1, lk1, lq2, lk2, subln_w, batch, seq, lambda_init):
    n, width = q.shape
    heads = width // HEAD_WIDTH
    vt_cols = vt.shape[2]
    tq = min(ATT_Q_BLOCK, seq)
    assert seq % tq == 0 and tq % vt_cols == 0
    nq = seq // tq
    rows = [jnp.pad(v.astype(F32), (0, HEAD_WIDTH - v.shape[0])) for v in (lq1, lk1, lq2, lk2, subln_w)]
    par = jnp.stack(rows + [jnp.zeros((HEAD_WIDTH,), F32)] * (8 - len(rows)))
    resident = pl.Buffered(1)
    return pl.pallas_call(
        functools.partial(_att_kernel, heads=heads, tq=tq, vt_cols=vt_cols, lambda_init=lambda_init),
        out_shape=jax.ShapeDtypeStruct((n, width), BF16),
        grid=(batch, nq),
        in_specs=[pl.BlockSpec((tq, width), lambda b, i: (b * nq + i, 0)),
                  pl.BlockSpec((seq, width), lambda b, i: (b, 0), pipeline_mode=resident),
                  pl.BlockSpec((seq // vt_cols, width, vt_cols), lambda b, i: (b, 0, 0),
                               pipeline_mode=resident),
                  pl.BlockSpec(par.shape, lambda b, i: (0, 0))],
        out_specs=pl.BlockSpec((tq, width), lambda b, i: (b * nq + i, 0)),
        scratch_shapes=[pltpu.VMEM((2 * tq, width), BF16),
                        pltpu.VMEM((heads, 1, 2 * tq), F32),
                        pltpu.VMEM((heads, HEAD_WIDTH + ATT_SUM_ROWS, 2 * tq), F32),
                        pltpu.VMEM((ATT_STAGES, tq, 2 * tq), F32),
                        pltpu.VMEM((ATT_STAGES, tq, 2 * tq), BF16)],
        compiler_params=_params(("parallel", "arbitrary")),
        name="diff_attention",
    )(q, k, vt, par)


def _rec_levels(t):
    return [t >> s for s in range(1, t.bit_length())]


def _rec_kernel(tri_ref, q_ref, lf_ref, kk_ref, v_ref, g_ref, gw_ref, o_ref, state_ref,
                b_scr, qd_scr, kd_scr, sc_scr, *, heads, t):
    @pl.when(pl.program_id(1) == 0)
    def _():
        state_ref[...] = jnp.zeros(state_ref.shape, F32)

    levels = _rec_levels(t)
    n_lv = len(levels)
    tok = lax.broadcasted_iota(jnp.int32, (t, HEAD_WIDTH), 0)
    row = lax.broadcasted_iota(jnp.int32, (t, t), 0)
    col = lax.broadcasted_iota(jnp.int32, (t, t), 1)
    split = row ^ col
    trans_b = (((1,), (1,)), ((), ()))
    head_cols = lambda h: slice(h * HEAD_WIDTH, (h + 1) * HEAD_WIDTH)

    def token_block(c, carry):
        rows = pl.ds(pl.multiple_of(c * t, t), t)

        def cumulative_decay(h):
            lf = lf_ref[rows, head_cols(h)]
            lf_hi = lf.astype(BF16)
            lf_lo = (lf - lf_hi.astype(F32)).astype(BF16)
            b2 = jnp.dot(tri_ref[...], jnp.concatenate([lf_hi, lf_lo], axis=1),
                         preferred_element_type=F32)
            b_scr[h] = b2[:, :HEAD_WIDTH] + b2[:, HEAD_WIDTH:]

        def level_exponent(h, w, lf):
            late = (tok & w) != 0
            if w == 1:
                return jnp.where(late, lf, 0.0)
            if w == 2:
                nxt = pltpu.roll(lf, t - 1, 0)
                prv = pltpu.roll(lf, 1, 0)
                odd = (tok & 1) != 0
                return jnp.where(late, jnp.where(odd, lf + prv, lf), jnp.where(odd, 0.0, nxt))
            d = jnp.concatenate([b_scr[h, r0:r0 + 2 * w] - b_scr[h, r0 + w - 1:r0 + w]
                                 for r0 in range(0, t, 2 * w)], axis=0)
            return jnp.where(late, d, -d)

        def operands(h):
            q, kk = q_ref[rows, head_cols(h)], kk_ref[rows, head_cols(h)]
            lf = lf_ref[rows, head_cols(h)]
            s = h % REC_STAGES
            qd_scr[s, 0] = q.astype(BF16)
            kd_scr[s, 0] = kk.astype(BF16)
            for idx, w in enumerate(levels):
                late = (tok & w) != 0
                mixed = jnp.where(late, q, kk) * jnp.exp(level_exponent(h, w, lf))
                qd_scr[s, idx + 1] = mixed.astype(BF16)
            b = b_scr[h]
            qd_scr[s, n_lv + 1] = (q * jnp.exp(b)).astype(BF16)
            kd_scr[s, 1] = (kk * jnp.exp(b_scr[h, t - 1:t] - b)).astype(BF16)

        def block_scores(h):
            s = h % REC_STAGES
            scores = lax.dot_general(qd_scr[s, 0], kd_scr[s, 0], trans_b,
                                     preferred_element_type=F32)
            for idx, w in reversed(list(enumerate(levels))):
                mixed = qd_scr[s, idx + 1]
                p = lax.dot_general(mixed, mixed, trans_b, preferred_element_type=F32)
                scores = jnp.where(split >= w, p, scores)
            sc_scr[h % 2] = jnp.where(row >= col, scores, 0.0).astype(BF16)

        def outputs(h):
            s = h % REC_STAGES
            v = v_ref[rows, head_cols(h)]
            st = state_ref[h]
            o = (lax.dot_general(qd_scr[s, n_lv + 1], st.astype(BF16), trans_b,
                                 preferred_element_type=F32)
                 + jnp.dot(sc_scr[h % 2], v, preferred_element_type=F32))
            upd = lax.dot_general(v, kd_scr[s, 1], (((0,), (0,)), ((), ())),
                                  preferred_element_type=F32)
            state_ref[h] = st * jnp.exp(b_scr[h, t - 1:t]) + upd
            o = _rms(o, gw_ref[...]) * g_ref[rows, head_cols(h)]
            o_ref[rows, head_cols(h)] = o.astype(o_ref.dtype)

        for h in range(heads):
            cumulative_decay(h)
        for h in range(min(2, heads)):
            operands(h)
        block_scores(0)
        for h in range(heads):
            if h + 2 < heads:
                operands(h + 2)
            if h + 1 < heads:
                block_scores(h + 1)
            outputs(h)
        return carry

    lax.fori_loop(0, q_ref.shape[0] // t, token_block, 0)


def _hgrn2(rq, lf, kk, ri, rg, gnorm_w, batch, seq):
    n, width = rq.shape
    heads = width // HEAD_WIDTH
    t = min(REC_BLOCK, seq)
    assert seq % t == 0 and t & (t - 1) == 0 and t >= 16
    ts = min(REC_STEP_TOKENS, seq)
    assert seq % ts == 0 and ts % t == 0
    nb = seq // ts
    tri =jnp.asarray(np.tril(np.ones((t, t), np.float32)), BF16)
    n_lv = len(_rec_levels(t))
    blk = pl.BlockSpec((ts, width), lambda b, c: (b * nb + c, 0))
    return pl.pallas_call(
        functools.partial(_rec_kernel, heads=heads, t=t),
        out_shape=jax.ShapeDtypeStruct((n, width), BF16),
        grid=(batch, nb),
        in_specs=[pl.BlockSpec((t, t), lambda b, c: (0, 0)), blk, blk, blk, blk, blk,
                  pl.BlockSpec((1, HEAD_WIDTH), lambda b, c: (0, 0))],
        out_specs=blk,
        scratch_shapes=[pltpu.VMEM((heads, HEAD_WIDTH, HEAD_WIDTH), F32),
                        pltpu.VMEM((heads, t, HEAD_WIDTH), F32),
                        pltpu.VMEM((REC_STAGES, n_lv + 2, t, HEAD_WIDTH), BF16),
                        pltpu.VMEM((REC_STAGES, 2, t, HEAD_WIDTH), BF16),
                        pltpu.VMEM((2, t, t), BF16)],
        compiler_params=_params(("parallel", "arbitrary")),
        name="hgrn2",
    )(tri, rq, lf, kk, ri, rg, gnorm_w.reshape(1, -1))


def _out_kernel(x_ref, att_ref, rec_ref, wa_ref, wr_ref, o_ref):
    o_ref[...] = (x_ref[...]
                  + jnp.dot(att_ref[...], wa_ref[...], preferred_element_type=F32)
                  + jnp.dot(rec_ref[...], wr_ref[...], preferred_element_type=F32))


def _out_proj(x, att, rec, w_out, layer):
    n, d = x.shape
    wa = att.shape[1]
    wr = rec.shape[1]
    tm = min(OUT_TOKENS, n)
    assert n % tm == 0
    row = lambda i: (i, 0)
    return pl.pallas_call(
        _out_kernel,
        out_shape=jax.ShapeDtypeStruct((n, d), F32),
        grid=(n // tm,),
        in_specs=[pl.BlockSpec((tm, d), row), pl.BlockSpec((tm, wa), row),
                  pl.BlockSpec((tm, wr), row),
                  pl.BlockSpec((None, None, wa, d), lambda i: (layer, 0, 0, 0)),
                  pl.BlockSpec((None, None, wr, d), lambda i: (layer, 0, 1, 0))],
        out_specs=pl.BlockSpec((tm, d), row),
        compiler_params=_params(("parallel",)),
        name="out_proj",
    )(x, att, rec, w_out, w_out)


def kernel(x, ffn1_norm, ffn1_w_gate, ffn1_w_up, ffn1_w_down, mix_norm, w_in, w_out, lambda_q1, lambda_k1, lambda_q2, lambda_k2, diff_subln, hgrn_lower_bounds, hgrn_gnorm, ffn2_norm, ffn2_w_gate, ffn2_w_up, ffn2_w_down, final_norm):
    batch, seq, d = x.shape
    depth = w_in.shape[0]
    assert w_out.shape[1] == 2 * w_in.shape[2] // N_IN_PARTS

    tables = _rope_tables(seq)
    lb_raw = hgrn_lower_bounds.astype(F32)
    tf = min(FFN_HIDDEN_BLOCK, ffn1_w_gate.shape[2])
    f1 = (_to_bf16_blocks(ffn1_w_gate, tf), _to_bf16_blocks(ffn1_w_up, tf),
          _to_bf16_blocks(ffn1_w_down))
    f2 = (_to_bf16_blocks(ffn2_w_gate, tf), _to_bf16_blocks(ffn2_w_up, tf),
          _to_bf16_blocks(ffn2_w_down))
    w_in_b = _to_bf16_blocks(w_in, w_in.shape[2] // N_IN_PARTS)
    w_out_b = _to_bf16_blocks(w_out)

    h = x.reshape(batch * seq, d)
    for l in range(depth):
        lambda_init = 0.8 - 0.6 * math.exp(-0.3 * l)
        h = _ffn(h, ffn1_norm[l], *f1, l)
        q, k, vt, rq, lf, kk, ri, rg = _proj(h, mix_norm[l], w_in_b, tables, lb_raw, l, seq)
        att = _attention(q, k, vt, lambda_q1[l], lambda_k1[l], lambda_q2[l], lambda_k2[l],
                         diff_subln[l], batch, seq, lambda_init)
        rec = _hgrn2(rq, lf, kk, ri, rg, hgrn_gnorm[l], batch, seq)
        h = _out_proj(h, att, rec, w_out_b, l)
        h = _ffn(h, ffn2_norm[l], *f2, l, final_w=final_norm if l == depth - 1 else None)
    return h.reshape(batch, seq, d)
```
